```python
import jax, jax.numpy as jnp
from jax import lax
import numpy as np

D_MODEL = 2048
BATCH = 2
SEQ = 4096
DEPTH = 1
DEC_BATCH = 128
DEC_SEQ = 1
PAST_LEN = 8192
PAGE_SIZE = 128

HEAD_DIM = 64
D_MIX = D_MODEL
A_HEADS = D_MIX // 2 // HEAD_DIM
B_HEADS = D_MIX // 2 // HEAD_DIM
B_KV_HEADS = B_HEADS // 4
A_WIDTH = A_HEADS * HEAD_DIM
B_WIDTH = B_HEADS * HEAD_DIM
B_KV_WIDTH = B_KV_HEADS * HEAD_DIM
DILATIONS = ((128, 1), (512, 4), (2048, 16))
A_REACH = max(w for w, _ in DILATIONS)
B_WINDOW = 128
BLK = 128
D_PLE = 256
EPS = 1e-6
IN_SPLITS = (A_WIDTH, A_WIDTH, A_WIDTH, A_WIDTH, B_WIDTH, B_KV_WIDTH, B_KV_WIDTH, B_WIDTH)
D_IN = sum(IN_SPLITS)

kernel_name = 'hymba_dilated_swa_sink_decoder_step'


def rms_norm(x, g):
    x32 = x.astype(jnp.float32)
    y = x32 * lax.rsqrt(jnp.mean(x32 * x32, axis=-1, keepdims=True) + EPS) * g.astype(jnp.float32)
    return y.astype(x.dtype)


def alibi_slopes(n):
    return jnp.exp2(-8.0 * jnp.arange(1, n + 1, dtype=jnp.float32) / n)


def softmax_lse(logits, sink):
    m = jnp.max(logits, axis=-1, keepdims=True)
    if sink is not None:
        m = jnp.maximum(m, sink)
    e = jnp.exp(logits - m)
    den = jnp.sum(e, axis=-1, keepdims=True)
    if sink is not None:
        den = den + jnp.exp(sink - m)
    return e / den, (m + jnp.log(den))[..., 0]


def banded_attention(q, k, v, step, n_keys, slopes, sinks):
    assert n_keys <= BLK
    N, L, H, hd = q.shape
    G = k.shape[2]
    R = H // G
    nb = -(-L // BLK)
    Lp = nb * BLK
    pad = lambda a: jnp.pad(a.astype(jnp.float32), ((0, 0), (0, Lp - L), (0, 0), (0, 0)))
    qb = pad(q).reshape(N, nb, BLK, G, R, hd) * (HEAD_DIM ** -0.5)
    kc = pad(k).reshape(N, nb, BLK, G, hd)
    vc = pad(v).reshape(N, nb, BLK, G, hd)
    with_prev = lambda c: jnp.concatenate(
        [jnp.concatenate([jnp.zeros_like(c[:, :1]), c[:, :-1]], axis=1), c], axis=2)
    kk, vv = with_prev(kc), with_prev(vc)
    s = jnp.einsum('nbqgrd,nbkgd->nbgrqk', qb, kk)
    blk = jnp.arange(nb)[:, None, None] * BLK
    qpos = blk + jnp.arange(BLK)[None, :, None]
    kpos = blk - BLK + jnp.arange(2 * BLK)[None, None, :]
    dist = qpos - kpos
    valid = (dist >= 0) & (dist <= n_keys) & (kpos >= 0)
    bias = -slopes.reshape(G, R)[None, :, :, None, None] * (dist * step).astype(jnp.float32)[:, None, None]
    logits = jnp.where(valid[:, None, None], s + bias, -jnp.inf)
    sink = None if sinks is None else sinks.astype(jnp.float32).reshape(G, R, 1, 1)
    p, lse = softmax_lse(logits, sink)
    o = jnp.einsum('nbgrqk,nbkgd->nbqgrd', p, vv).reshape(N, Lp, H, hd)[:, :L]
    lse = lse.transpose(0, 1, 4, 2, 3).reshape(N, Lp, H)[:, :L]
    return o, lse


def dilated_prompt(q, k, v, step, n_keys, slopes):
    B, S, H, hd = q.shape
    Sp = -(-S // step) * step
    def to_sub(a):
        a = jnp.pad(a, ((0, 0), (0, Sp - S), (0, 0), (0, 0)))
        return a.reshape(B, Sp // step, step, *a.shape[2:]).transpose(0, 2, 1, 3, 4).reshape(B * step, Sp // step, *a.shape[2:])
    o, lse = banded_attention(to_sub(q), to_sub(k), to_sub(v), step, n_keys, slopes, None)
    o = o.reshape(B, step, Sp // step, H, hd).transpose(0, 2, 1, 3, 4).reshape(B, Sp, H, hd)[:, :S]
    lse = lse.reshape(B, step, Sp // step, H).transpose(0, 2, 1, 3).reshape(B, Sp, H)[:, :S]
    return o, lse


def gathered_attention(q, k_ctx, v_ctx, n_past, step, n_keys, slopes, sinks):
    N, T, H, hd = q.shape
    G = k_ctx.shape[2]
    R = H // G
    m = jnp.arange(n_keys + 1)
    idx = n_past + jnp.arange(T)[:, None] - m[None, :] * step
    valid = idx >= 0
    idx = jnp.maximum(idx, 0)
    kg = jnp.take(k_ctx, idx, axis=1).astype(jnp.float32)
    vg = jnp.take(v_ctx, idx, axis=1).astype(jnp.float32)
    qg = q.astype(jnp.float32).reshape(N, T, G, R, hd) * (HEAD_DIM ** -0.5)
    s = jnp.einsum('ntgrd,ntmgd->ntgrm', qg, kg)
    bias = -slopes.reshape(G, R, 1) * (m * step).astype(jnp.float32)
    logits = jnp.where(valid[None, :, None, None, :], s + bias, -jnp.inf)
    sink = None if sinks is None else sinks.astype(jnp.float32).reshape(G, R, 1)
    p, lse = softmax_lse(logits, sink)
    o = jnp.einsum('ntgrm,ntmgd->ntgrd', p, vg).reshape(N, T, H, hd)
    return o, lse.reshape(N, T, H)


def combine_patterns(results):
    outs = jnp.stack([o for o, _ in results])
    w = jax.nn.softmax(jnp.stack([l for _, l in results]), axis=0)
    return jnp.sum(w[..., None] * outs, axis=0)


def mixer_inputs(x, g_mix, w_in):
    z = rms_norm(x, g_mix) @ w_in
    qa, ka, va, ga, qb, kb, vb, gb = jnp.split(z, list(np.cumsum(IN_SPLITS)[:-1]), axis=-1)
    heads = lambda t: t.reshape(*t.shape[:-1], t.shape[-1] // HEAD_DIM, HEAD_DIM)
    return heads(qa), heads(ka), heads(va), ga, heads(qb), heads(kb), heads(vb), gb


def mixer_output(x, oa, ob, ga, gb, p, w_out, g_ple, w_pg, w_ple):
    ya = oa.reshape(*oa.shape[:-2], A_WIDTH).astype(x.dtype) * jax.nn.silu(ga)
    yb = ob.reshape(*ob.shape[:-2], B_WIDTH).astype(x.dtype) * jax.nn.silu(gb)
    h = x + jnp.concatenate([ya, yb], axis=-1) @ w_out
    gate = jax.nn.sigmoid(rms_norm(h, g_ple) @ w_pg)
    return h + (p @ w_ple) * gate


def setup_inputs(seed: int = 0) -> dict:
    key = jax.random.key(seed)
    ks = jax.random.split(key, 16)
    f32 = jnp.float32
    a_buf = min(A_REACH, PAST_LEN)
    b_buf = min(B_WINDOW, PAST_LEN)
    nrm = lambda k, shape: jax.random.normal(k, shape, f32)
    return {
        'x_prompt': nrm(ks[0], (BATCH, SEQ, D_MODEL)),
        'x_sample': nrm(ks[1], (DEC_BATCH, DEC_SEQ, D_MODEL)),
        'cache_a_k': nrm(ks[2], (DEPTH, DEC_BATCH, a_buf, A_HEADS, HEAD_DIM)),
        'cache_a_v': nrm(ks[3], (DEPTH, DEC_BATCH, a_buf, A_HEADS, HEAD_DIM)),
        'cache_b_k': nrm(ks[4], (DEPTH, DEC_BATCH, b_buf, B_KV_HEADS, HEAD_DIM)),
        'cache_b_v': nrm(ks[5], (DEPTH, DEC_BATCH, b_buf, B_KV_HEADS, HEAD_DIM)),
        'p_prompt': nrm(ks[6], (DEPTH, BATCH, SEQ, D_PLE)),
        'p_sample': nrm(ks[7], (DEPTH, DEC_BATCH, DEC_SEQ, D_PLE)),
        'g_mix': 1.0 + 0.1 * nrm(ks[8], (DEPTH, D_MODEL)),
        'w_in': nrm(ks[9], (DEPTH, D_MODEL, D_IN)) * D_MODEL ** -0.5,
        'sinks': 0.5 * nrm(ks[10], (DEPTH, B_HEADS)),
        'w_out': nrm(ks[11], (DEPTH, A_WIDTH + B_WIDTH, D_MODEL)) * (A_WIDTH + B_WIDTH) ** -0.5,
        'g_ple': 1.0 + 0.1 * nrm(ks[12], (DEPTH, D_MODEL)),
        'w_pg': nrm(ks[13], (DEPTH, D_MODEL, D_MODEL)) * D_MODEL ** -0.5,
        'w_ple': nrm(ks[14], (DEPTH, D_PLE, D_MODEL)) * D_PLE ** -0.5,
        'g_final': 1.0 + 0.1 * nrm(ks[15], (D_MODEL,)),
    }


def reference(x_prompt, x_sample, cache_a_k, cache_a_v, cache_b_k, cache_b_v, p_prompt, p_sample,
              g_mix, w_in, sinks, w_out, g_ple, w_pg, w_ple, g_final):
    slopes_a = alibi_slopes(A_HEADS)
    slopes_b = alibi_slopes(B_HEADS)
    n_past_a = cache_a_k.shape[2]
    n_past_b = cache_b_k.shape[2]
    xp, xs = x_prompt, x_sample
    ak_p, av_p, bk_p, bv_p, ak_s, av_s, bk_s, bv_s = [], [], [], [], [], [], [], []
    for i in range(DEPTH):
        qa, ka, va, ga, qb, kb, vb, gb = mixer_inputs(xp, g_mix[i], w_in[i])
        oa = combine_patterns([dilated_prompt(qa, ka, va, d, w // d, slopes_a) for w, d in DILATIONS])
        ob, _ = banded_attention(qb, kb, vb, 1, B_WINDOW, slopes_b, sinks[i])
        ak_p.append(ka[:, -A_REACH:]); av_p.append(va[:, -A_REACH:])
        bk_p.append(kb[:, -B_WINDOW:]); bv_p.append(vb[:, -B_WINDOW:])
        xp = mixer_output(xp, oa, ob, ga, gb, p_prompt[i], w_out[i], g_ple[i], w_pg[i], w_ple[i])

        qa, ka, va, ga, qb, kb, vb, gb = mixer_inputs(xs, g_mix[i], w_in[i])
        ka_ctx = jnp.concatenate([cache_a_k[i].astype(ka.dtype), ka], axis=1)
        va_ctx = jnp.concatenate([cache_a_v[i].astype(va.dtype), va], axis=1)
        kb_ctx = jnp.concatenate([cache_b_k[i].astype(kb.dtype), kb], axis=1)
        vb_ctx = jnp.concatenate([cache_b_v[i].astype(vb.dtype), vb], axis=1)
        oa = combine_patterns([gathered_attention(qa, ka_ctx, va_ctx, n_past_a, d, w // d, slopes_a, None)
                               for w, d in DILATIONS])
        ob, _ = gathered_attention(qb, kb_ctx, vb_ctx, n_past_b, 1, B_WINDOW, slopes_b, sinks[i])
        ak_s.append(ka); av_s.append(va); bk_s.append(kb); bv_s.append(vb)
        xs = mixer_output(xs, oa, ob, ga, gb, p_sample[i], w_out[i], g_ple[i], w_pg[i], w_ple[i])
    y_prompt = rms_norm(xp, g_final)
    y_sample = rms_norm(xs, g_final)
    return (y_prompt, y_sample,
            jnp.stack(ak_p), jnp.stack(av_p), jnp.stack(bk_p), jnp.stack(bv_p),
            jnp.stack(ak_s), jnp.stack(av_s), jnp.stack(bk_s), jnp.stack(bv_s))
```

```python
import functools
import math

import jax
import jax.numpy as jnp
from jax import lax
from jax.experimental import pallas as pl
from jax.experimental.pallas import tpu as pltpu

F32 = jnp.float32
BF16 = jnp.bfloat16

LANES = 128
SUBLANES = 8
HEAD_DIM = 64
A_HEADS = 16
B_HEADS = 16
B_KV_HEADS = 4
BLK = 128
N_KEYS = 128
DILATIONS = ((128, 1), (512, 4), (2048, 16))
A_REACH = 2048
B_WINDOW = 128
EPS = 1e-6
NEG = -1e30

A_WIDTH = A_HEADS * HEAD_DIM
B_WIDTH = B_HEADS * HEAD_DIM
B_KV_WIDTH = B_KV_HEADS * HEAD_DIM
IN_SPLITS = (A_WIDTH, A_WIDTH, A_WIDTH, A_WIDTH, B_WIDTH, B_KV_WIDTH, B_KV_WIDTH, B_WIDTH)
D_IN = sum(IN_SPLITS)
N_CB = D_IN // LANES
QA0, KA0, VA0, GA0, QB0, KB0, VB0, GB0 = (sum(IN_SPLITS[:i]) for i in range(8))
A_CB = A_WIDTH // LANES
B_CB = B_WIDTH // LANES
KV_CB = B_KV_WIDTH // LANES
Q_PER_KV = B_HEADS // B_KV_HEADS

PROJ_TN = 512
B_CHUNK = 1024
VMEM_LIMIT = 56 * 1024 * 1024


def _cparams(sem):
    return pltpu.CompilerParams(dimension_semantics=sem, vmem_limit_bytes=VMEM_LIMIT)


def _rms(x, g):
    return x * lax.rsqrt(jnp.mean(x * x, axis=-1, keepdims=True) + EPS) * g


def _silu(g):
    return g * jax.nn.sigmoid(g)


def _proj_kernel(x_ref, g_ref, w_ref, zf_ref, zb_ref, hn_ref):
    j = pl.program_id(1)

    @pl.when(j == 0)
    def _():
        hn_ref[...] = _rms(x_ref[...], g_ref[...]).astype(BF16)

    acc = jnp.dot(hn_ref[...], w_ref[...], preferred_element_type=F32)
    col = j * PROJ_TN
    is_q = ((col >= QA0) & (col < QA0 + A_WIDTH)) | ((col >= QB0) & (col < QB0 + B_WIDTH))
    acc = acc * jnp.where(is_q, HEAD_DIM ** -0.5, 1.0)
    for c in range(PROJ_TN // LANES):
        blk = acc[:, c * LANES:(c + 1) * LANES]
        zf_ref[c] = blk
        zb_ref[c] = blk.astype(BF16)


def _proj(x, g, w_bf, tm):
    rows, d = x.shape
    out_spec = pl.BlockSpec((PROJ_TN // LANES, tm, LANES), lambda i, j: (j, i, 0))
    return pl.pallas_call(
        _proj_kernel,
        grid=(rows // tm, D_IN // PROJ_TN),
        in_specs=[pl.BlockSpec((tm, d), lambda i, j: (i, 0)),
                  pl.BlockSpec((1, d), lambda i, j: (0, 0)),
                  pl.BlockSpec((d, PROJ_TN), lambda i, j: (0, j))],
        out_specs=[out_spec, out_spec],
        out_shape=[jax.ShapeDtypeStruct((N_CB, rows, LANES), F32),
                   jax.ShapeDtypeStruct((N_CB, rows, LANES), BF16)],
        scratch_shapes=[pltpu.VMEM((tm, d), BF16)],
        compiler_params=_cparams(("parallel", "arbitrary")),
        name="proj",
    )(x, g.reshape(1, d), w_bf)


def _band_bias(slope, step):
    qi = lax.broadcasted_iota(jnp.int32, (BLK, 2 * BLK), 0)
    kj = lax.broadcasted_iota(jnp.int32, (BLK, 2 * BLK), 1)
    dist = qi + BLK - kj
    valid = (dist >= 0) & (dist <= N_KEYS)
    return jnp.where(valid, -slope * (dist * step).astype(F32), NEG)


def _head_slope(h, n_heads):
    return jnp.exp2(jnp.full((1, 1), -8.0 / n_heads, F32) * (h + 1).astype(F32))


def _softmax_block(q, kk, vv, bias, sink=None):
    s = lax.dot_general(q, kk, (((1,), (1,)), ((), ())), preferred_element_type=F32) + bias
    m = jnp.max(s, axis=1, keepdims=True)
    if sink is not None:
        m = jnp.maximum(m, sink)
    p = jnp.exp(s - m)
    den = jnp.sum(p, axis=1, keepdims=True)
    if sink is not None:
        den = den + jnp.exp(sink - m)
    o = jnp.dot(p.astype(BF16), vv, preferred_element_type=F32)
    return o / den, m + jnp.log(den)


def _merge(o_a, l_a, o_b, l_b):
    m = jnp.maximum(l_a, l_b)
    w_a = jnp.exp(l_a - m)
    w_b = jnp.exp(l_b - m)
    den = w_a + w_b
    return (w_a * o_a + w_b * o_b) / den, m + jnp.log(den)


def _attn_a_kernel(q1_ref, k1_ref, v1_ref, q4_ref, k4_ref, v4_ref, q16_ref, k16_ref, v16_ref, g16_ref,
                   y16_ref,
                   q4_s, k4_s, v4_s, q16_s, k16_s, v16_s, g16_s, y16_s,
                   o1_s, l1_s, o4_s, l4_s, bias_s, *, seq):
    hp = pl.program_id(1)
    lo = lax.broadcasted_iota(jnp.int32, (BLK, LANES), 1) < HEAD_DIM

    for pi, (_, step) in enumerate(DILATIONS):
        for hh in range(2):
            bias_s[pi, hh] = _band_bias(_head_slope(2 * hp + hh, A_HEADS), step)

    for r in range(4):
        sl = slice(r * LANES, (r + 1) * LANES)
        q4_s[r] = q4_ref[:, sl]
        k4_s[r] = k4_ref[:, sl]
        v4_s[r] = v4_ref[:, sl]
    for r in range(16):
        sl = slice(r * LANES, (r + 1) * LANES)
        q16_s[r] = q16_ref[:, sl]
        k16_s[r] = k16_ref[:, sl]
        v16_s[r] = v16_ref[:, sl]
        g16_s[r] = g16_ref[:, sl]

    def block(pi, load, row0, first):
        q = load("q", row0, BLK)
        if first:
            kk, vv = load("k", row0, BLK), load("v", row0, BLK)
        else:
            kk, vv = load("k", row0 - BLK, 2 * BLK), load("v", row0 - BLK, 2 * BLK)
        outs = []
        for hh in range(2):
            qm = jnp.where(lo if hh == 0 else ~lo, q, jnp.zeros_like(q))
            bias = bias_s[pi, hh, :, BLK:] if first else bias_s[pi, hh]
            outs.append(_softmax_block(qm, kk, vv, bias))
        o = jnp.where(lo, outs[0][0], outs[1][0])
        lse = jnp.where(lo, outs[0][1], outs[1][1])
        return o, lse

    def load1(name, row0, n):
        ref = {"q": q1_ref, "k": k1_ref, "v": v1_ref}[name]
        return ref[pl.ds(row0, n), :]

    def pass1(row0, first):
        o, lse = block(0, load1, row0, first)
        o1_s[pl.ds(row0, BLK), :] = o
        l1_s[pl.ds(row0, BLK), :] = lse

    pass1(0, True)

    def body1(b, c):
        pass1(pl.multiple_of(b * BLK, BLK), False)
        return c
    lax.fori_loop(1, seq // BLK, body1, 0)

    def pass4(r, row0, first):
        def load4(name, a, n):
            ref = {"q": q4_s, "k": k4_s, "v": v4_s}[name]
            return ref[r, pl.ds(a, n), :]
        o, lse = block(1, load4, row0, first)
        prev = pl.ds(r + 4 * row0, BLK, stride=4)
        o, lse = _merge(o1_s[prev, :], l1_s[prev, :], o, lse)
        o4_s[r, pl.ds(row0, BLK), :] = o
        l4_s[r, pl.ds(row0, BLK), :] = lse

    def body4a(r, c):
        pass4(r, 0, True)
        return c
    lax.fori_loop(0, 4, body4a, 0)

    nb4 = seq // 4 // BLK

    def body4b(i, c):
        pass4(i // (nb4 - 1), pl.multiple_of((i % (nb4 - 1) + 1) * BLK, BLK), False)
        return c
    lax.fori_loop(0, 4 * (nb4 - 1), body4b, 0)

    def pass16(r, row0, first):
        def load16(name, a, n):
            ref = {"q": q16_s, "k": k16_s, "v": v16_s}[name]
            return ref[r, pl.ds(a, n), :]
        o, lse = block(2, load16, row0, first)
        prev = pl.ds(r // 4 + 4 * row0, BLK, stride=4)
        o, _ = _merge(o4_s[r % 4, prev, :], l4_s[r % 4, prev, :], o, lse)
        y16_s[r, pl.ds(row0, BLK), :] = (o * _silu(g16_s[r, pl.ds(row0, BLK), :])).astype(BF16)

    def body16a(r, c):
        pass16(r, 0, True)
        return c
    lax.fori_loop(0, 16, body16a, 0)

    nb16 = seq // 16 // BLK

    def body16b(i, c):
        pass16(i // (nb16 - 1), pl.multiple_of((i % (nb16 - 1) + 1) * BLK, BLK), False)
        return c
    lax.fori_loop(0, 16 * (nb16 - 1), body16b, 0)

    for r in range(16):
        y16_ref[:, r * LANES:(r + 1) * LANES] = y16_s[r]


def _attn_a(zf, zb, batch, seq):
    assert DILATIONS == ((N_KEYS, 1), (4 * N_KEYS, 4), (16 * N_KEYS, 16)) and N_KEYS == BLK
    assert seq % (16 * BLK) == 0 and seq // 16 >= 2 * BLK
    views = [zb.reshape(N_CB, batch, seq // d, d * LANES) for d in (1, 4, 16)]
    g16 = zf.reshape(N_CB, batch, seq // 16, 16 * LANES)

    def spec(d, off):
        return pl.BlockSpec((None, None, seq // d, d * LANES), lambda b, hp: (off // LANES + hp, b, 0, 0))

    in_specs = [spec(d, off) for d in (1, 4, 16) for off in (QA0, KA0, VA0)] + [spec(16, GA0)]
    res = lambda d, dt: pltpu.VMEM((d, seq // d, LANES), dt)
    y = pl.pallas_call(
        functools.partial(_attn_a_kernel, seq=seq),
        grid=(batch, A_CB),
        in_specs=in_specs,
        out_specs=pl.BlockSpec((None, None, seq // 16, 16 * LANES), lambda b, hp: (hp, b, 0, 0)),
        out_shape=jax.ShapeDtypeStruct((A_CB, batch, seq // 16, 16 * LANES), BF16),
        scratch_shapes=[res(4, BF16), res(4, BF16), res(4, BF16),
                        res(16, BF16), res(16, BF16), res(16, BF16), res(16, F32), res(16, BF16),
                        pltpu.VMEM((seq, LANES), F32), pltpu.VMEM((seq, LANES), F32),
                        res(4, F32), res(4, F32),
                        pltpu.VMEM((3, 2, BLK, 2 * BLK), F32)],
        compiler_params=_cparams(("parallel", "parallel")),
        name="attn_a",
    )(views[0], views[0], views[0], views[1], views[1], views[1], views[2], views[2], views[2], g16)
    return y.reshape(A_CB, batch * seq, LANES)


def _attn_b_kernel(sink_ref, q_ref, kc_ref, kp_ref, vc_ref, vp_ref, g_ref, y_ref, kk_s, vv_s, bias_s):
    kvc = pl.program_id(1)
    ch = pl.program_id(2)
    lo = lax.broadcasted_iota(jnp.int32, (BLK, LANES), 1) < HEAD_DIM
    heads = 2 * Q_PER_KV

    for cur, prev, dst in ((kc_ref, kp_ref, kk_s), (vc_ref, vp_ref, vv_s)):
        for src, a, n in ((prev, 0, BLK), (cur, BLK, B_CHUNK)):
            x = src[...]
            dst[0, a:a + n, :] = x
            dst[1, a:a + n, :] = pltpu.roll(x.astype(F32), HEAD_DIM, 1).astype(BF16)

    in_prev = lax.broadcasted_iota(jnp.int32, (BLK, 2 * BLK), 1) < BLK
    for hl in range(heads):
        full = _band_bias(_head_slope(kvc * heads + hl, B_HEADS), 1)
        bias_s[0, hl] = full
        bias_s[1, hl] = jnp.where(in_prev, NEG, full)

    def body(i, c):
        row0 = pl.multiple_of(i * BLK, BLK)
        first = ((ch == 0) & (i == 0)).astype(jnp.int32)
        for qc in range(heads // 2):
            q = q_ref[qc, pl.ds(row0, BLK), :]
            outs = []
            for half in range(2):
                hl = 2 * qc + half
                var = 0 if hl // Q_PER_KV == half else 1
                qm = jnp.where(lo if half == 0 else ~lo, q, jnp.zeros_like(q))
                kk = kk_s[var, pl.ds(row0, 2 * BLK), :]
                vv = vv_s[var, pl.ds(row0, 2 * BLK), :]
                sink = sink_ref[kvc * heads + hl]
                outs.append(_softmax_block(qm, kk, vv, bias_s[first, hl], sink=sink)[0])
            o = jnp.where(lo, outs[0], outs[1])
            y_ref[qc, pl.ds(row0, BLK), :] = (o * _silu(g_ref[qc, pl.ds(row0, BLK), :])).astype(BF16)
        return c
    lax.fori_loop(0, B_CHUNK // BLK, body, 0)


def _attn_b(zf, zb, sinks, batch, seq):
    assert seq % B_CHUNK == 0 and B_WINDOW == BLK
    nch = seq // B_CHUNK
    qpk = B_CB // KV_CB
    row = lambda b, c: b * nch + c
    prev = lambda b, c: jnp.maximum((b * nch + c) * (B_CHUNK // BLK) - 1, 0)
    cur_spec = lambda off: pl.BlockSpec((None, B_CHUNK, LANES),
                                        lambda b, kvc, c: (off // LANES + kvc, row(b, c), 0))
    prev_spec = lambda off: pl.BlockSpec((None, BLK, LANES),
                                         lambda b, kvc, c: (off // LANES + kvc, prev(b, c), 0))
    wide = lambda off: pl.BlockSpec((qpk, B_CHUNK, LANES),
                                    lambda b, kvc, c: (off // LANES // qpk + kvc, row(b, c), 0))
    return pl.pallas_call(
        _attn_b_kernel,
        grid=(batch, KV_CB, nch),
        in_specs=[pl.BlockSpec(memory_space=pltpu.SMEM),
                  wide(QB0), cur_spec(KB0), prev_spec(KB0), cur_spec(VB0), prev_spec(VB0), wide(GB0)],
        out_specs=pl.BlockSpec((qpk, B_CHUNK, LANES), lambda b, kvc, c: (kvc, row(b, c), 0)),
        out_shape=jax.ShapeDtypeStruct((B_CB, batch * seq, LANES), BF16),
        scratch_shapes=[pltpu.VMEM((2, BLK + B_CHUNK, LANES), BF16),
                        pltpu.VMEM((2, BLK + B_CHUNK, LANES), BF16),
                        pltpu.VMEM((2, 2 * Q_PER_KV, BLK, 2 * BLK), F32)],
        compiler_params=_cparams(("parallel", "parallel", "arbitrary")),
        name="attn_b",
    )(sinks, zb, zb, zb, zb, zb, zf)


PAD_ROWS = SUBLANES


def _decode_kernel(z_ref, k1_ref, k4_ref, k16_ref, v1_ref, v4_ref, v16_ref, kb_ref, vb_ref,
                   bias_a_ref, bias_b_ref, sink_ref, y_ref):
    z = z_ref[0]
    head_of_lane = lax.broadcasted_iota(jnp.int32, (A_HEADS, A_WIDTH), 1) // HEAD_DIM
    sel = head_of_lane == lax.broadcasted_iota(jnp.int32, (A_HEADS, A_WIDTH), 0)
    expand = sel.astype(BF16)

    def attend(q, keys, vals, bias, sink):
        qbd = jnp.where(sel, jnp.broadcast_to(q, sel.shape), 0.0).astype(BF16)
        s = lax.dot_general(keys.astype(BF16), qbd, (((1,), (1,)), ((), ())),
                            preferred_element_type=F32) + bias
        m = jnp.max(s, axis=0, keepdims=True)
        if sink is not None:
            m = jnp.maximum(m, sink)
        p = jnp.exp(s - m)
        den = jnp.sum(p, axis=0, keepdims=True)
        if sink is not None:
            den = den + jnp.exp(sink - m)
        pe = jnp.dot((p / den).astype(BF16), expand, preferred_element_type=F32)
        return jnp.sum(pe * vals, axis=0, keepdims=True)

    def with_new(cache_rows, new_row):
        return jnp.concatenate(cache_rows + [jnp.broadcast_to(new_row, (PAD_ROWS, new_row.shape[1]))], axis=0)

    def widen(x):
        lo = lax.broadcasted_iota(jnp.int32, (x.shape[0], LANES), 1) < HEAD_DIM
        cols = []
        for g in range(B_KV_HEADS):
            src = x[:, (g // 2) * LANES:(g // 2 + 1) * LANES]
            swapped = pltpu.roll(src, HEAD_DIM, 1)
            both = jnp.where(lo, src, swapped) if g % 2 == 0 else jnp.where(lo, swapped, src)
            cols += [both] * (Q_PER_KV * HEAD_DIM // LANES)
        return jnp.concatenate(cols, axis=1)

    seg = lambda off, width: z[:, off:off + width]
    ka = with_new([k1_ref[...], k4_ref[...], k16_ref[...]], seg(KA0, A_WIDTH))
    va = with_new([v1_ref[...], v4_ref[...], v16_ref[...]], seg(VA0, A_WIDTH))
    oa = attend(seg(QA0, A_WIDTH), ka, va, bias_a_ref[...], None)
    kb = widen(with_new([kb_ref[...]], seg(KB0, B_KV_WIDTH)))
    vb = widen(with_new([vb_ref[...]], seg(VB0, B_KV_WIDTH)))
    ob = attend(seg(QB0, B_WIDTH), kb, vb, bias_b_ref[...], sink_ref[...])
    y_ref[0, :, :A_WIDTH] = (oa * _silu(seg(GA0, A_WIDTH))).astype(BF16)
    y_ref[0, :, A_WIDTH:] = (ob * _silu(seg(GB0, B_WIDTH))).astype(BF16)


def _decode_bias(n_heads, steps, self_count):
    slopes = jnp.exp2(-8.0 * jnp.arange(1, n_heads + 1, dtype=F32) / n_heads)
    back = (N_KEYS - jnp.arange(N_KEYS)).astype(F32)
    rows = [-(back[:, None] * step) * slopes[None, :] for step in steps]
    rows.append(jnp.full((1, n_heads), math.log(self_count), F32))
    rows.append(jnp.full((PAD_ROWS - 1, n_heads), NEG, F32))
    return jnp.concatenate(rows, axis=0)


def _decode(z_s, cache_a_k, cache_a_v, cache_b_k, cache_b_v, sinks):
    n = z_s.shape[0]
    assert cache_a_k.shape[1] == A_REACH and cache_b_k.shape[1] == B_WINDOW == N_KEYS
    steps = tuple(d for _, d in DILATIONS)
    a_views = lambda c: [c.reshape(n, A_REACH // d, d * A_WIDTH) for d in steps]
    a_specs = [pl.BlockSpec((None, N_KEYS, A_WIDTH), functools.partial(lambda i, blk: (i, blk, 0),
                                                                       blk=A_REACH // d // N_KEYS - 1))
               for d in steps]
    b_spec = pl.BlockSpec((None, B_WINDOW, B_KV_WIDTH), lambda i: (i, 0, 0))
    bias_a = _decode_bias(A_HEADS, steps, len(steps))
    bias_b = _decode_bias(B_HEADS, (1,), 1)
    const = lambda a: pl.BlockSpec(a.shape, lambda i: (0,) * a.ndim)
    sink2 = sinks.reshape(1, B_HEADS)
    y = pl.pallas_call(
        _decode_kernel,
        grid=(n,),
        in_specs=[pl.BlockSpec((1, 1, D_IN), lambda i: (i, 0, 0))] + a_specs + a_specs + [b_spec, b_spec]
                 + [const(bias_a), const(bias_b), const(sink2)],
        out_specs=pl.BlockSpec((1, 1, A_WIDTH + B_WIDTH), lambda i: (i, 0, 0)),
        out_shape=jax.ShapeDtypeStruct((n, 1, A_WIDTH + B_WIDTH), BF16),
        compiler_params=_cparams(("parallel",)),
        name="decode",
    )(z_s.reshape(n, 1, D_IN), *a_views(cache_a_k), *a_views(cache_a_v),
      cache_b_k.reshape(n, B_WINDOW, B_KV_WIDTH), cache_b_v.reshape(n, B_WINDOW, B_KV_WIDTH),
      bias_a, bias_b, sink2)
    return y.reshape(n, A_WIDTH + B_WIDTH)


def _out_kernel(x_ref, ya_ref, yb_ref, p_ref, wout_ref, gple_ref, wpg_ref, wple_ref, gfin_ref, o_ref):
    y = jnp.concatenate([ya_ref[c] for c in range(A_CB)] + [yb_ref[c] for c in range(B_CB)], axis=1)
    h = x_ref[...] + jnp.dot(y, wout_ref[...], preferred_element_type=F32)
    gate = jax.nn.sigmoid(jnp.dot(_rms(h, gple_ref[...]).astype(BF16), wpg_ref[...],
                                  preferred_element_type=F32))
    ple = jnp.dot(p_ref[...].astype(BF16), wple_ref[...], preferred_element_type=F32)
    o_ref[...] = _rms(h + ple * gate, gfin_ref[...])


def _out(x, ya, yb, p, wout_bf, g_ple, wpg_bf, wple_bf, g_final, tm):
    rows, d = x.shape
    resident = lambda a: pl.BlockSpec(a.shape, lambda i: (0,) * a.ndim, pipeline_mode=pl.Buffered(1))
    g_ple, g_final = g_ple.reshape(1, d), g_final.reshape(1, d)
    return pl.pallas_call(
        _out_kernel,
        grid=(rows // tm,),
        in_specs=[pl.BlockSpec((tm, d), lambda i: (i, 0)),
                  pl.BlockSpec((A_CB, tm, LANES), lambda i: (0, i, 0)),
                  pl.BlockSpec((B_CB, tm, LANES), lambda i: (0, i, 0)),
                  pl.BlockSpec((tm, p.shape[1]), lambda i: (i, 0)),
                  resident(wout_bf), resident(g_ple), resident(wpg_bf), resident(wple_bf), resident(g_final)],
        out_specs=pl.BlockSpec((tm, d), lambda i: (i, 0)),
        out_shape=jax.ShapeDtypeStruct((rows, d), F32),
        compiler_params=_cparams(("parallel",)),
        name="out",
    )(x, ya, yb, p, wout_bf, g_ple, wpg_bf, wple_bf, g_final)


def _rows_to_heads(zf, off, width, batch, seq, keep):
    cb = zf[off // LANES:(off + width) // LANES].reshape(width // LANES, batch, seq, LANES)[:, :, seq - keep:]
    return cb.transpose(1, 2, 0, 3).reshape(batch, keep, width // HEAD_DIM, HEAD_DIM)


def kernel(x_prompt, x_sample, cache_a_k, cache_a_v, cache_b_k, cache_b_v, p_prompt, p_sample,
           g_mix, w_in, sinks, w_out, g_ple, w_pg, w_ple, g_final):
    depth = w_in.shape[0]
    batch, seq, d = x_prompt.shape
    n_s, t_s, _ = x_sample.shape
    assert depth == 1 and t_s == 1 and w_in.shape[2] == D_IN
    xp = x_prompt.reshape(batch * seq, d)
    xs = x_sample.reshape(n_s, d)
    i = 0
    w_in_bf, w_out_bf, w_pg_bf, w_ple_bf = (w[i].astype(BF16) for w in (w_in, w_out, w_pg, w_ple))

    zf, zb = _proj(xp, g_mix[i], w_in_bf, tm=1024)
    ya = _attn_a(zf, zb, batch, seq)
    yb = _attn_b(zf, zb, sinks[i], batch, seq)
    y_prompt = _out(xp, ya, yb, p_prompt[i].reshape(batch * seq, -1), w_out_bf, g_ple[i], w_pg_bf, w_ple_bf,
                    g_final, tm=256)
    new_p = [_rows_to_heads(zf, off, width, batch, seq, keep)[None]
             for off, width, keep in ((KA0, A_WIDTH, min(A_REACH, seq)), (VA0, A_WIDTH, min(A_REACH, seq)),
                                      (KB0, B_KV_WIDTH, min(B_WINDOW, seq)), (VB0, B_KV_WIDTH, min(B_WINDOW, seq)))]

    zf_s, _ = _proj(xs, g_mix[i], w_in_bf, tm=n_s)
    z_s = zf_s.transpose(1, 0, 2).reshape(n_s, D_IN)
    y_s = _decode(z_s, cache_a_k[i], cache_a_v[i], cache_b_k[i], cache_b_v[i], sinks[i])
    y_cb = y_s.reshape(n_s, A_CB + B_CB, LANES).transpose(1, 0, 2)
    y_sample = _out(xs, y_cb[:A_CB], y_cb[A_CB:], p_sample[i].reshape(n_s, -1), w_out_bf, g_ple[i], w_pg_bf,
                    w_ple_bf, g_final, tm=n_s)
    new_s = [z_s[:, off:off + width].reshape(1, n_s, 1, width // HEAD_DIM, HEAD_DIM)
             for off, width in ((KA0, A_WIDTH), (VA0, A_WIDTH), (KB0, B_KV_WIDTH), (VB0, B_KV_WIDTH))]

    return (y_prompt.reshape(batch, seq, d), y_sample.reshape(n_s, t_s, d), *new_p, *new_s)
```

```python
import functools
import math

import jax
import jax.numpy as jnp
from jax import lax
from jax.experimental import pallas as pl
from jax.experimental.pallas import tpu as pltpu

F32 = jnp.float32
BF16 = jnp.bfloat16

LANES = 128
SUBLANES = 8
HEAD_DIM = 64
A_HEADS = 16
B_HEADS = 16
B_KV_HEADS = 4
BLK = 128
N_KEYS = 128
DILATIONS = ((128, 1), (512, 4), (2048, 16))
A_REACH = 2048
B_WINDOW = 128
EPS = 1e-6
NEG = -1e30

A_WIDTH = A_HEADS * HEAD_DIM
B_WIDTH = B_HEADS * HEAD_DIM
B_KV_WIDTH = B_KV_HEADS * HEAD_DIM
IN_SPLITS = (A_WIDTH, A_WIDTH, A_WIDTH, A_WIDTH, B_WIDTH, B_KV_WIDTH, B_KV_WIDTH, B_WIDTH)
D_IN = sum(IN_SPLITS)
N_CB = D_IN // LANES
QA0, KA0, VA0, GA0, QB0, KB0, VB0, GB0 = (sum(IN_SPLITS[:i]) for i in range(8))
A_CB = A_WIDTH // LANES
B_CB = B_WIDTH // LANES
KV_CB = B_KV_WIDTH // LANES
Q_PER_KV = B_HEADS // B_KV_HEADS

PROJ_TN = 512
B_CHUNK = 1024
A_UNITS = 4
DEC_CHUNK = 16
VMEM_LIMIT = 56 * 1024 * 1024


def _cparams(sem):
    return pltpu.CompilerParams(dimension_semantics=sem, vmem_limit_bytes=VMEM_LIMIT)


def _rms(x, g):
    return x * lax.rsqrt(jnp.mean(x * x, axis=-1, keepdims=True) + EPS) * g


def _silu(g):
    return g * jax.nn.sigmoid(g)


def _proj_kernel(x_ref, g_ref, w_ref, z_ref, hn_ref):
    j = pl.program_id(1)

    @pl.when(j == 0)
    def _():
        hn_ref[...] = _rms(x_ref[...], g_ref[...]).astype(BF16)

    acc = jnp.dot(hn_ref[...], w_ref[...], preferred_element_type=F32)
    col = j * PROJ_TN
    is_q = ((col >= QA0) & (col < QA0 + A_WIDTH)) | ((col >= QB0) & (col < QB0 + B_WIDTH))
    acc = acc * jnp.where(is_q, HEAD_DIM ** -0.5, 1.0)
    for c in range(PROJ_TN // LANES):
        z_ref[c] = acc[:, c * LANES:(c + 1) * LANES]


def _proj(x, g, w_bf, tm):
    rows, d = x.shape
    return pl.pallas_call(
        _proj_kernel,
        grid=(rows // tm, D_IN // PROJ_TN),
        in_specs=[pl.BlockSpec((tm, d), lambda i, j: (i, 0)),
                  pl.BlockSpec((1, d), lambda i, j: (0, 0)),
                  pl.BlockSpec((d, PROJ_TN), lambda i, j: (0, j))],
        out_specs=pl.BlockSpec((PROJ_TN // LANES, tm, LANES), lambda i, j: (j, i, 0)),
        out_shape=jax.ShapeDtypeStruct((N_CB, rows, LANES), F32),
        scratch_shapes=[pltpu.VMEM((tm, d), BF16)],
        compiler_params=_cparams(("parallel", "arbitrary")),
        name="proj",
    )(x, g.reshape(1, d), w_bf)


def _band_bias(slope, step):
    qi = lax.broadcasted_iota(jnp.int32, (BLK, 2 * BLK), 0)
    kj = lax.broadcasted_iota(jnp.int32, (BLK, 2 * BLK), 1)
    dist = qi + BLK - kj
    valid = (dist >= 0) & (dist <= N_KEYS)
    return jnp.where(valid, -slope * (dist * step).astype(F32), NEG)


def _head_slope(h, n_heads):
    return jnp.exp2(jnp.full((1, 1), -8.0 / n_heads, F32) * (h + 1).astype(F32))


def _softmax_block(q, kk, vv, bias, sink=None):
    s = lax.dot_general(q, kk, (((1,), (1,)), ((), ())), preferred_element_type=F32) + bias
    m = jnp.max(s, axis=1, keepdims=True)
    if sink is not None:
        m = jnp.maximum(m, sink)
    p = jnp.exp(s - m)
    den = jnp.sum(p, axis=1, keepdims=True)
    if sink is not None:
        den = den + jnp.exp(sink - m)
    o = jnp.dot(p.astype(BF16), vv, preferred_element_type=F32)
    return o / den, m + jnp.log(den)


def _attn_a_kernel(q_ref, k_ref, v_ref, g_ref, y_ref,
                   q1_s, k1_s, v1_s, q4_s, k4_s, v4_s, q16_s, k16_s, v16_s,
                   st16_s, st4_s, bias_s, ones_s, *, seq):
    hp = pl.program_id(1)
    lo = lax.broadcasted_iota(jnp.int32, (BLK, LANES), 1) < HEAD_DIM
    low_lanes = lambda rows: lax.broadcasted_iota(jnp.int32, (rows, LANES), 1) < HEAD_DIM
    steps = tuple(d for _, d in DILATIONS)

    for pi, step in enumerate(steps):
        for hh in range(2):
            bias_s[pi, hh * BLK:(hh + 1) * BLK, :] = _band_bias(_head_slope(2 * hp + hh, A_HEADS), step)
    ones_s[0] = jnp.where(low_lanes(2 * BLK), 1.0, 0.0).astype(BF16)
    ones_s[1] = jnp.where(low_lanes(2 * BLK), 0.0, 1.0).astype(BF16)

    for src, dsts in ((q_ref, (q1_s, q4_s, q16_s)), (k_ref, (k1_s, k4_s, k16_s)), (v_ref, (v1_s, v4_s, v16_s))):
        dsts[0][0] = src[...].astype(BF16)
        for dst, d in zip(dsts[1:], steps[1:]):
            for r in range(d):
                dst[r] = src[pl.ds(r, seq // d, stride=d), :].astype(BF16)

    def scores(pi, refs, r, row0, first):
        q_s, k_s, v_s = refs
        q = q_s[r, pl.ds(row0, BLK), :]
        q2 = jnp.concatenate([jnp.where(lo, q, jnp.zeros_like(q)), jnp.where(lo, jnp.zeros_like(q), q)], axis=0)
        if first:
            keys, bias = pl.ds(row0, BLK), bias_s[pi, :, BLK:]
        else:
            keys, bias = pl.ds(row0 - BLK, 2 * BLK), bias_s[pi]
        s = lax.dot_general(q2, k_s[r, keys, :], (((1,), (1,)), ((), ())), preferred_element_type=F32) + bias
        return s, v_s[r, keys, :]

    def attend(s, vv):
        nk = vv.shape[0]
        halves = (s[:BLK], s[BLK:])
        m = [jnp.max(h, axis=1, keepdims=True) for h in halves]
        lhs = jnp.concatenate([jnp.exp(h - mh).astype(BF16) for h, mh in zip(halves, m)], axis=1)
        zero = jnp.zeros_like(vv)
        rhs = jnp.concatenate(
            [jnp.concatenate([jnp.where(low_lanes(nk), vv, zero), ones_s[0, :nk, :]], axis=1),
             jnp.concatenate([jnp.where(low_lanes(nk), zero, vv), ones_s[1, :nk, :]], axis=1)], axis=0)
        ol = jnp.dot(lhs, rhs, preferred_element_type=F32)
        mt = jnp.where(lo, jnp.broadcast_to(m[0], (BLK, LANES)), jnp.broadcast_to(m[1], (BLK, LANES)))
        return ol[:, :LANES], ol[:, LANES:], mt

    def merge(a, b):
        m = jnp.maximum(a[2], b[2])
        wa = jnp.exp(a[2] - m)
        wb = jnp.exp(b[2] - m)
        return wa * a[0] + wb * b[0], wa * a[1] + wb * b[1], m

    def run(pi, refs, units, finish):
        results = []
        nxt = scores(pi, refs, *units[0])
        for k, unit in enumerate(units):
            s, vv = nxt
            if k + 1 < len(units):
                nxt = scores(pi, refs, *units[k + 1])
            results.append(attend(s, vv))
        for unit, res in zip(units, results):
            finish(unit, res)

    def blocks(pi, refs, n_res, finish):
        nb = seq // steps[pi] // BLK

        def first_trip(i, c):
            run(pi, refs, [(i * A_UNITS + u, 0, True) for u in range(A_UNITS)], finish)
            return c

        def trip(i, c):
            units = []
            for u in range(A_UNITS):
                idx = i * A_UNITS + u
                units.append((idx // (nb - 1), pl.multiple_of((idx % (nb - 1) + 1) * BLK, BLK), False))
            run(pi, refs, units, finish)
            return c

        if n_res >= A_UNITS:
            lax.fori_loop(0, n_res // A_UNITS, first_trip, 0)
            lax.fori_loop(0, n_res * (nb - 1) // A_UNITS, trip, 0)
        else:
            run(pi, refs, [(0, 0, True)] + [(0, b * BLK, False) for b in range(1, A_UNITS)], finish)

            def trip1(i, c):
                run(pi, refs, [(0, pl.multiple_of((i * A_UNITS + u) * BLK, BLK), False) for u in range(A_UNITS)],
                    finish)
                return c
            lax.fori_loop(1, nb // A_UNITS, trip1, 0)

    def finish16(unit, res):
        r, row0, _ = unit
        rows = pl.ds(r // 4 + 4 * row0, BLK, stride=4)
        for c in range(3):
            st16_s[c, r % 4, rows, :] = res[c]
    blocks(2, (q16_s, k16_s, v16_s), 16, finish16)

    def finish4(unit, res):
        r, row0, _ = unit
        res = merge(tuple(st16_s[c, r, pl.ds(row0, BLK), :] for c in range(3)), res)
        rows = pl.ds(r + 4 * row0, BLK, stride=4)
        for c in range(3):
            st4_s[c, rows, :] = res[c]
    blocks(1, (q4_s, k4_s, v4_s), 4, finish4)

    def finish1(unit, res):
        _, row0, _ = unit
        rows = pl.ds(row0, BLK)
        o, l, _ = merge(tuple(st4_s[c, rows, :] for c in range(3)), res)
        y_ref[rows, :] = (o / l * _silu(g_ref[rows, :])).astype(BF16)
    blocks(0, (q1_s, k1_s, v1_s), 1, finish1)


def _attn_a(z, batch, seq):
    assert DILATIONS == ((N_KEYS, 1), (4 * N_KEYS, 4), (16 * N_KEYS, 16)) and N_KEYS == BLK
    assert seq % (16 * BLK) == 0 and seq // 16 >= 2 * BLK and 4 % A_UNITS == 0 and seq // BLK % A_UNITS == 0
    spec = lambda off: pl.BlockSpec((None, seq, LANES), lambda b, hp: (off // LANES + hp, b, 0))
    res = lambda d, dt: pltpu.VMEM((d, seq // d, LANES), dt)
    return pl.pallas_call(
        functools.partial(_attn_a_kernel, seq=seq),
        grid=(batch, A_CB),
        in_specs=[spec(QA0), spec(KA0), spec(VA0), spec(GA0)],
        out_specs=pl.BlockSpec((None, seq, LANES), lambda b, hp: (hp, b, 0)),
        out_shape=jax.ShapeDtypeStruct((A_CB, batch * seq, LANES), BF16),
        scratch_shapes=[res(d, BF16) for d in (1, 1, 1, 4, 4, 4, 16, 16, 16)]
                       + [pltpu.VMEM((3, 4, seq // 4, LANES), F32), pltpu.VMEM((3, seq, LANES), F32),
                          pltpu.VMEM((3, 2 * BLK, 2 * BLK), F32), pltpu.VMEM((2, 2 * BLK, LANES), BF16)],
        compiler_params=_cparams(("parallel", "parallel")),
        name="attn_a",
    )(z, z, z, z)


def _attn_b_kernel(sink_ref, q_ref, kc_ref, kp_ref, vc_ref, vp_ref, g_ref, y_ref, kk_s, vv_s, bias_s):
    kvc = pl.program_id(1)
    ch = pl.program_id(2)
    lo = lax.broadcasted_iota(jnp.int32, (BLK, LANES), 1) < HEAD_DIM
    heads = 2 * Q_PER_KV

    for cur, prev, dst in ((kc_ref, kp_ref, kk_s), (vc_ref, vp_ref, vv_s)):
        for src, a, n in ((prev, 0, BLK), (cur, BLK, B_CHUNK)):
            x = src[...]
            dst[0, a:a + n, :] = x.astype(BF16)
            dst[1, a:a + n, :] = pltpu.roll(x, HEAD_DIM, 1).astype(BF16)

    in_prev = lax.broadcasted_iota(jnp.int32, (BLK, 2 * BLK), 1) < BLK
    for hl in range(heads):
        full = _band_bias(_head_slope(kvc * heads + hl, B_HEADS), 1)
        bias_s[0, hl] = full
        bias_s[1, hl] = jnp.where(in_prev, NEG, full)

    def body(i, c):
        row0 = pl.multiple_of(i * BLK, BLK)
        first = ((ch == 0) & (i == 0)).astype(jnp.int32)
        for qc in range(heads // 2):
            q = q_ref[qc, pl.ds(row0, BLK), :].astype(BF16)
            outs = []
            for half in range(2):
                hl = 2 * qc + half
                var = 0 if hl // Q_PER_KV == half else 1
                qm = jnp.where(lo if half == 0 else ~lo, q, jnp.zeros_like(q))
                kk = kk_s[var, pl.ds(row0, 2 * BLK), :]
                vv = vv_s[var, pl.ds(row0, 2 * BLK), :]
                sink = sink_ref[kvc * heads + hl]
                outs.append(_softmax_block(qm, kk, vv, bias_s[first, hl], sink=sink)[0])
            o = jnp.where(lo, outs[0], outs[1])
            y_ref[qc, pl.ds(row0, BLK), :] = (o * _silu(g_ref[qc, pl.ds(row0, BLK), :])).astype(BF16)
        return c
    lax.fori_loop(0, B_CHUNK // BLK, body, 0)


def _attn_b(z, sinks, batch, seq):
    assert seq % B_CHUNK == 0 and B_WINDOW == BLK
    nch = seq // B_CHUNK
    qpk = B_CB // KV_CB
    row = lambda b, c: b * nch + c
    prev = lambda b, c: jnp.maximum((b * nch + c) * (B_CHUNK // BLK) - 1, 0)
    cur_spec = lambda off: pl.BlockSpec((None, B_CHUNK, LANES),
                                        lambda b, kvc, c: (off // LANES + kvc, row(b, c), 0))
    prev_spec = lambda off: pl.BlockSpec((None, BLK, LANES),
                                         lambda b, kvc, c: (off // LANES + kvc, prev(b, c), 0))
    wide = lambda off: pl.BlockSpec((qpk, B_CHUNK, LANES),
                                    lambda b, kvc, c: (off // LANES // qpk + kvc, row(b, c), 0))
    return pl.pallas_call(
        _attn_b_kernel,
        grid=(batch, KV_CB, nch),
        in_specs=[pl.BlockSpec(memory_space=pltpu.SMEM),
                  wide(QB0), cur_spec(KB0), prev_spec(KB0), cur_spec(VB0), prev_spec(VB0), wide(GB0)],
        out_specs=pl.BlockSpec((qpk, B_CHUNK, LANES), lambda b, kvc, c: (kvc, row(b, c), 0)),
        out_shape=jax.ShapeDtypeStruct((B_CB, batch * seq, LANES), BF16),
        scratch_shapes=[pltpu.VMEM((2, BLK + B_CHUNK, LANES), BF16),
                        pltpu.VMEM((2, BLK + B_CHUNK, LANES), BF16),
                        pltpu.VMEM((2, 2 * Q_PER_KV, BLK, 2 * BLK), F32)],
        compiler_params=_cparams(("parallel", "parallel", "arbitrary")),
        name="attn_b",
    )(sinks, z, z, z, z, z, z)


PAD_ROWS = SUBLANES


def _decode_kernel(z_ref, qa_ref, kna_ref, vna_ref, ga_ref, k1_ref, k4_ref, k16_ref, v1_ref, v4_ref, v16_ref,
                   kb_ref, vb_ref, bias_a_ref, bias_b_ref, sink_ref, ya_ref, yb_ref, s_scr):
    q = qa_ref[...]
    ones = jnp.ones((HEAD_DIM, LANES), BF16)
    lane_sum = lambda x: jnp.dot(x.astype(BF16), ones, preferred_element_type=F32)
    k_refs, v_refs = (k1_ref, k4_ref, k16_ref), (v1_ref, v4_ref, v16_ref)
    n_trips = N_KEYS // DEC_CHUNK

    s_self = lane_sum(q * kna_ref[...]) + math.log(len(k_refs))
    m = s_self
    for vi, k_ref in enumerate(k_refs):
        prod = (k_ref[...] * q[None]).reshape(N_KEYS * A_HEADS, HEAD_DIM)
        s = lane_sum(prod).reshape(N_KEYS, A_HEADS, LANES) + bias_a_ref[vi]
        s_scr[vi] = s
        m = jnp.maximum(m, jnp.max(s, axis=0))
    p_self = jnp.exp(s_self - m)
    den, acc = p_self, p_self[:, :HEAD_DIM] * vna_ref[...]
    for vi, v_ref in enumerate(v_refs):
        def body(c, carry, vi=vi, v_ref=v_ref):
            den, acc = carry
            rows = pl.ds(pl.multiple_of(c * DEC_CHUNK, DEC_CHUNK), DEC_CHUNK)
            p = jnp.exp(s_scr[vi, rows] - m[None])
            return den + jnp.sum(p, axis=0), acc + jnp.sum(p[:, :, :HEAD_DIM] * v_ref[rows], axis=0)
        den, acc = lax.fori_loop(0, n_trips, body, (den, acc))
    ya_ref[...] = (acc / den[:, :HEAD_DIM] * _silu(ga_ref[...])).astype(BF16)

    z = z_ref[0]
    head_of_lane = lax.broadcasted_iota(jnp.int32, (B_HEADS, B_WIDTH), 1) // HEAD_DIM
    sel = head_of_lane == lax.broadcasted_iota(jnp.int32, (B_HEADS, B_WIDTH), 0)
    expand = sel.astype(BF16)

    def with_new(cache_rows, new_row):
        return jnp.concatenate([cache_rows, jnp.broadcast_to(new_row, (PAD_ROWS, new_row.shape[1]))], axis=0)

    def widen(x):
        lo = lax.broadcasted_iota(jnp.int32, (x.shape[0], LANES), 1) < HEAD_DIM
        cols = []
        for g in range(B_KV_HEADS):
            src = x[:, (g // 2) * LANES:(g // 2 + 1) * LANES]
            swapped = pltpu.roll(src, HEAD_DIM, 1)
            both = jnp.where(lo, src, swapped) if g % 2 == 0 else jnp.where(lo, swapped, src)
            cols += [both] * (Q_PER_KV * HEAD_DIM // LANES)
        return jnp.concatenate(cols, axis=1)

    seg = lambda off, width: z[:, off:off + width]
    keys = widen(with_new(kb_ref[...], seg(KB0, B_KV_WIDTH)))
    vals = widen(with_new(vb_ref[...], seg(VB0, B_KV_WIDTH)))
    qbd = jnp.where(sel, jnp.broadcast_to(seg(QB0, B_WIDTH), sel.shape), 0.0).astype(BF16)
    s = lax.dot_general(keys.astype(BF16), qbd, (((1,), (1,)), ((), ())),
                        preferred_element_type=F32) + bias_b_ref[...]
    sink = sink_ref[...]
    mb = jnp.maximum(jnp.max(s, axis=0, keepdims=True), sink)
    p = jnp.exp(s - mb)
    denb = jnp.sum(p, axis=0, keepdims=True) + jnp.exp(sink - mb)
    pe = jnp.dot((p / denb).astype(BF16), expand, preferred_element_type=F32)
    ob = jnp.sum(pe * vals, axis=0, keepdims=True)
    yb_ref[0] = (ob * _silu(seg(GB0, B_WIDTH))).astype(BF16)


def _decode(z_s, cache_a_k, cache_a_v, cache_b_k, cache_b_v, sinks):
    n = z_s.shape[0]
    assert cache_a_k.shape[1] == A_REACH and cache_b_k.shape[1] == B_WINDOW == N_KEYS
    steps = tuple(d for _, d in DILATIONS)
    slopes = lambda nh: jnp.exp2(-8.0 * jnp.arange(1, nh + 1, dtype=F32) / nh)
    back = (N_KEYS - jnp.arange(N_KEYS)).astype(F32)
    a_views = lambda c: [c.reshape(n, A_REACH // d, d, A_HEADS, HEAD_DIM) for d in steps]
    a_specs = [pl.BlockSpec((None, N_KEYS, None, A_HEADS, HEAD_DIM),
                            functools.partial(lambda i, blk: (i, blk, 0, 0, 0), blk=A_REACH // d // N_KEYS - 1))
               for d in steps]
    bias_a = jnp.stack([-(back[:, None] * d) * slopes(A_HEADS)[None, :] for d in steps])
    bias_a = jnp.broadcast_to(bias_a[..., None], bias_a.shape + (LANES,))
    bias_b = jnp.concatenate([-back[:, None] * slopes(B_HEADS)[None, :], jnp.zeros((1, B_HEADS), F32),
                              jnp.full((PAD_ROWS - 1, B_HEADS), NEG, F32)], axis=0)
    heads = lambda off: z_s[:, off:off + A_WIDTH].reshape(n, A_HEADS, HEAD_DIM)
    head_spec = pl.BlockSpec((None, A_HEADS, HEAD_DIM), lambda i: (i, 0, 0))
    b_spec = pl.BlockSpec((None, B_WINDOW, B_KV_WIDTH), lambda i: (i, 0, 0))
    const = lambda a: pl.BlockSpec(a.shape, lambda i: (0,) * a.ndim)
    sink2 = sinks.reshape(1, B_HEADS)
    ya, yb = pl.pallas_call(
        _decode_kernel,
        grid=(n,),
        in_specs=[pl.BlockSpec((1, 1, D_IN), lambda i: (i, 0, 0))] + [head_spec] * 4 + a_specs + a_specs
                 + [b_spec, b_spec, const(bias_a), const(bias_b), const(sink2)],
        out_specs=[head_spec, pl.BlockSpec((1, 1, B_WIDTH), lambda i: (i, 0, 0))],
        out_shape=[jax.ShapeDtypeStruct((n, A_HEADS, HEAD_DIM), BF16),
                   jax.ShapeDtypeStruct((n, 1, B_WIDTH), BF16)],
        scratch_shapes=[pltpu.VMEM((len(steps), N_KEYS, A_HEADS, LANES), F32)],
        compiler_params=_cparams(("parallel",)),
        name="decode",
    )(z_s.reshape(n, 1, D_IN), heads(QA0), heads(KA0), heads(VA0), heads(GA0),
      *a_views(cache_a_k), *a_views(cache_a_v),
      cache_b_k.reshape(n, B_WINDOW, B_KV_WIDTH), cache_b_v.reshape(n, B_WINDOW, B_KV_WIDTH),
      bias_a, bias_b, sink2)
    return ya.reshape(n, A_WIDTH), yb.reshape(n, B_WIDTH)


def _out_kernel(x_ref, ya_ref, yb_ref, p_ref, wout_ref, gple_ref, wpg_ref, wple_ref, gfin_ref, o_ref):
    y = jnp.concatenate([ya_ref[c] for c in range(A_CB)] + [yb_ref[c] for c in range(B_CB)], axis=1)
    h = x_ref[...] + jnp.dot(y, wout_ref[...], preferred_element_type=F32)
    gate = jax.nn.sigmoid(jnp.dot(_rms(h, gple_ref[...]).astype(BF16), wpg_ref[...],
                                  preferred_element_type=F32))
    ple = jnp.dot(p_ref[...].astype(BF16), wple_ref[...], preferred_element_type=F32)
    o_ref[...] = _rms(h + ple * gate, gfin_ref[...])


def _out(x, ya, yb, p, wout_bf, g_ple, wpg_bf, wple_bf, g_final, tm):
    rows, d = x.shape
    resident = lambda a: pl.BlockSpec(a.shape, lambda i: (0,) * a.ndim, pipeline_mode=pl.Buffered(1))
    g_ple, g_final = g_ple.reshape(1, d), g_final.reshape(1, d)
    return pl.pallas_call(
        _out_kernel,
        grid=(rows // tm,),
        in_specs=[pl.BlockSpec((tm, d), lambda i: (i, 0)),
                  pl.BlockSpec((A_CB, tm, LANES), lambda i: (0, i, 0)),
                  pl.BlockSpec((B_CB, tm, LANES), lambda i: (0, i, 0)),
                  pl.BlockSpec((tm, p.shape[1]), lambda i: (i, 0)),
                  resident(wout_bf), resident(g_ple), resident(wpg_bf), resident(wple_bf), resident(g_final)],
        out_specs=pl.BlockSpec((tm, d), lambda i: (i, 0)),
        out_shape=jax.ShapeDtypeStruct((rows, d), F32),
        compiler_params=_cparams(("parallel",)),
        name="out",
    )(x, ya, yb, p, wout_bf, g_ple, wpg_bf, wple_bf, g_final)


def _rows_to_heads(z, off, width, batch, seq, keep):
    cb = z[off // LANES:(off + width) // LANES].reshape(width // LANES, batch, seq, LANES)[:, :, seq - keep:]
    return cb.transpose(1, 2, 0, 3).reshape(batch, keep, width // HEAD_DIM, HEAD_DIM)


def _col_blocks(y):
    return y.reshape(y.shape[0], y.shape[1] // LANES, LANES).transpose(1, 0, 2)


def kernel(x_prompt, x_sample, cache_a_k, cache_a_v, cache_b_k, cache_b_v, p_prompt, p_sample,
           g_mix, w_in, sinks, w_out, g_ple, w_pg, w_ple, g_final):
    depth = w_in.shape[0]
    batch, seq, d = x_prompt.shape
    n_s, t_s, _ = x_sample.shape
    assert depth == 1 and t_s == 1 and w_in.shape[2] == D_IN
    xp = x_prompt.reshape(batch * seq, d)
    xs = x_sample.reshape(n_s, d)
    i = 0
    w_in_bf, w_out_bf, w_pg_bf, w_ple_bf = (w[i].astype(BF16) for w in (w_in, w_out, w_pg, w_ple))

    z = _proj(xp, g_mix[i], w_in_bf, tm=1024)
    ya = _attn_a(z, batch, seq)
    yb = _attn_b(z, sinks[i], batch, seq)
    y_prompt = _out(xp, ya, yb, p_prompt[i].reshape(batch * seq, -1), w_out_bf, g_ple[i], w_pg_bf, w_ple_bf,
                    g_final, tm=256)
    new_p = [_rows_to_heads(z, off, width, batch, seq, keep)[None]
             for off, width, keep in ((KA0, A_WIDTH, min(A_REACH, seq)), (VA0, A_WIDTH, min(A_REACH, seq)),
                                      (KB0, B_KV_WIDTH, min(B_WINDOW, seq)), (VB0, B_KV_WIDTH, min(B_WINDOW, seq)))]

    z_s = _proj(xs, g_mix[i], w_in_bf, tm=n_s).transpose(1, 0, 2).reshape(n_s, D_IN)
    ya_s, yb_s = _decode(z_s, cache_a_k[i], cache_a_v[i], cache_b_k[i], cache_b_v[i], sinks[i])
    y_sample = _out(xs, _col_blocks(ya_s), _col_blocks(yb_s), p_sample[i].reshape(n_s, -1), w_out_bf, g_ple[i],
                    w_pg_bf, w_ple_bf, g_final, tm=n_s)
    new_s = [z_s[:, off:off + width].reshape(1, n_s, 1, width // HEAD_DIM, HEAD_DIM)
             for off, width in ((KA0, A_WIDTH), (VA0, A_WIDTH), (KB0, B_KV_WIDTH), (VB0, B_KV_WIDTH))]

    return (y_prompt.reshape(batch, seq, d), y_sample.reshape(n_s, t_s, d), *new_p, *new_s)
```

```python
import functools
import math

import jax
import jax.numpy as jnp
from jax import lax
from jax.experimental import pallas as pl
from jax.experimental.pallas import tpu as pltpu

F32 = jnp.float32
BF16 = jnp.bfloat16

LANES = 128
SUBLANES = 8
HEAD_DIM = 64
A_HEADS = 16
B_HEADS = 16
B_KV_HEADS = 4
BLK = 128
N_KEYS = 128
DILATIONS = ((128, 1), (512, 4), (2048, 16))
A_REACH = 2048
B_WINDOW = 128
EPS = 1e-6
NEG = -1e30

A_WIDTH = A_HEADS * HEAD_DIM
B_WIDTH = B_HEADS * HEAD_DIM
B_KV_WIDTH = B_KV_HEADS * HEAD_DIM
IN_SPLITS = (A_WIDTH, A_WIDTH, A_WIDTH, A_WIDTH, B_WIDTH, B_KV_WIDTH, B_KV_WIDTH, B_WIDTH)
D_IN = sum(IN_SPLITS)
N_CB = D_IN // LANES
QA0, KA0, VA0, GA0, QB0, KB0, VB0, GB0 = (sum(IN_SPLITS[:i]) for i in range(8))
A_CB = A_WIDTH // LANES
B_CB = B_WIDTH // LANES
KV_CB = B_KV_WIDTH // LANES
Q_PER_KV = B_HEADS // B_KV_HEADS

PROJ_TN = 512
B_CHUNK = 1024
A_UNITS = 4
DEC_CHUNK = 16
VMEM_LIMIT = 56 * 1024 * 1024


def _cparams(sem):
    return pltpu.CompilerParams(dimension_semantics=sem, vmem_limit_bytes=VMEM_LIMIT)


def _rms(x, g):
    return x * lax.rsqrt(jnp.mean(x * x, axis=-1, keepdims=True) + EPS) * g


def _silu(g):
    return g * jax.nn.sigmoid(g)


def _proj_kernel(x_ref, g_ref, w_ref, z_ref, hn_ref):
    j = pl.program_id(1)

    @pl.when(j == 0)
    def _():
        hn_ref[...] = _rms(x_ref[...], g_ref[...]).astype(BF16)

    acc = jnp.dot(hn_ref[...], w_ref[...], preferred_element_type=F32)
    col = j * PROJ_TN
    is_q = ((col >= QA0) & (col < QA0 + A_WIDTH)) | ((col >= QB0) & (col < QB0 + B_WIDTH))
    acc = acc * jnp.where(is_q, HEAD_DIM ** -0.5, 1.0)
    for c in range(PROJ_TN // LANES):
        z_ref[c] = acc[:, c * LANES:(c + 1) * LANES]


def _proj(x, g, w_bf, tm):
    rows, d = x.shape
    return pl.pallas_call(
        _proj_kernel,
        grid=(rows // tm, D_IN // PROJ_TN),
        in_specs=[pl.BlockSpec((tm, d), lambda i, j: (i, 0)),
                  pl.BlockSpec((1, d), lambda i, j: (0, 0)),
                  pl.BlockSpec((d, PROJ_TN), lambda i, j: (0, j))],
        out_specs=pl.BlockSpec((PROJ_TN // LANES, tm, LANES), lambda i, j: (j, i, 0)),
        out_shape=jax.ShapeDtypeStruct((N_CB, rows, LANES), F32),
        scratch_shapes=[pltpu.VMEM((tm, d), BF16)],
        compiler_params=_cparams(("parallel", "arbitrary")),
        name="proj",
    )(x, g.reshape(1, d), w_bf)


def _band_bias(slope, step):
    qi = lax.broadcasted_iota(jnp.int32, (BLK, 2 * BLK), 0)
    kj = lax.broadcasted_iota(jnp.int32, (BLK, 2 * BLK), 1)
    dist = qi + BLK - kj
    valid = (dist >= 0) & (dist <= N_KEYS)
    return jnp.where(valid, -slope * (dist * step).astype(F32), NEG)


def _head_slope(h, n_heads):
    return jnp.exp2(jnp.full((1, 1), -8.0 / n_heads, F32) * (h + 1).astype(F32))


def _softmax_block(q, kk, vv, bias, sink=None):
    s = lax.dot_general(q, kk, (((1,), (1,)), ((), ())), preferred_element_type=F32) + bias
    m = jnp.max(s, axis=1, keepdims=True)
    if sink is not None:
        m = jnp.maximum(m, sink)
    p = jnp.exp(s - m)
    den = jnp.sum(p, axis=1, keepdims=True)
    if sink is not None:
        den = den + jnp.exp(sink - m)
    o = jnp.dot(p.astype(BF16), vv, preferred_element_type=F32)
    return o / den, m + jnp.log(den)


def _attn_a_kernel(q_ref, k_ref, v_ref, g_ref, y_ref,
                   q1_s, k1_s, v1_s, q4_s, k4_s, v4_s, q16_s, k16_s, v16_s,
                   st16_s, st4_s, bias_s, ones_s, *, seq):
    hp = pl.program_id(1)
    lo = lax.broadcasted_iota(jnp.int32, (BLK, LANES), 1) < HEAD_DIM
    low_lanes = lambda rows: lax.broadcasted_iota(jnp.int32, (rows, LANES), 1) < HEAD_DIM
    steps = tuple(d for _, d in DILATIONS)

    for pi, step in enumerate(steps):
        for hh in range(2):
            bias_s[pi, hh * BLK:(hh + 1) * BLK, :] = _band_bias(_head_slope(2 * hp + hh, A_HEADS), step)
    ones_s[0] = jnp.where(low_lanes(2 * BLK), 1.0, 0.0).astype(BF16)
    ones_s[1] = jnp.where(low_lanes(2 * BLK), 0.0, 1.0).astype(BF16)

    for src, dsts in ((q_ref, (q1_s, q4_s, q16_s)), (k_ref, (k1_s, k4_s, k16_s)), (v_ref, (v1_s, v4_s, v16_s))):
        dsts[0][0] = src[...].astype(BF16)
        for dst, d in zip(dsts[1:], steps[1:]):
            for r in range(d):
                dst[r] = src[pl.ds(r, seq // d, stride=d), :].astype(BF16)

    def scores(pi, refs, r, row0, first):
        q_s, k_s, v_s = refs
        q = q_s[r, pl.ds(row0, BLK), :]
        q2 = jnp.concatenate([jnp.where(lo, q, jnp.zeros_like(q)), jnp.where(lo, jnp.zeros_like(q), q)], axis=0)
        if first:
            keys, bias = pl.ds(row0, BLK), bias_s[pi, :, BLK:]
        else:
            keys, bias = pl.ds(row0 - BLK, 2 * BLK), bias_s[pi]
        s = lax.dot_general(q2, k_s[r, keys, :], (((1,), (1,)), ((), ())), preferred_element_type=F32) + bias
        return s, v_s[r, keys, :]

    def attend(s, vv):
        nk = vv.shape[0]
        halves = (s[:BLK], s[BLK:])
        m = [jnp.max(h, axis=1, keepdims=True) for h in halves]
        lhs = jnp.concatenate([jnp.exp(h - mh).astype(BF16) for h, mh in zip(halves, m)], axis=1)
        zero = jnp.zeros_like(vv)
        rhs = jnp.concatenate(
            [jnp.concatenate([jnp.where(low_lanes(nk), vv, zero), ones_s[0, :nk, :]], axis=1),
             jnp.concatenate([jnp.where(low_lanes(nk), zero, vv), ones_s[1, :nk, :]], axis=1)], axis=0)
        ol = jnp.dot(lhs, rhs, preferred_element_type=F32)
        mt = jnp.where(lo, jnp.broadcast_to(m[0], (BLK, LANES)), jnp.broadcast_to(m[1], (BLK, LANES)))
        return ol[:, :LANES], ol[:, LANES:], mt

    def merge(a, b):
        m = jnp.maximum(a[2], b[2])
        wa = jnp.exp(a[2] - m)
        wb = jnp.exp(b[2] - m)
        return wa * a[0] + wb * b[0], wa * a[1] + wb * b[1], m

    def run(pi, refs, units, finish):
        results = []
        nxt = scores(pi, refs, *units[0])
        for k, unit in enumerate(units):
            s, vv = nxt
            if k + 1 < len(units):
                nxt = scores(pi, refs, *units[k + 1])
            results.append(attend(s, vv))
        for unit, res in zip(units, results):
            finish(unit, res)

    def blocks(pi, refs, n_res, finish):
        nb = seq // steps[pi] // BLK

        def first_trip(i, c):
            run(pi, refs, [(i * A_UNITS + u, 0, True) for u in range(A_UNITS)], finish)
            return c

        def trip(i, c):
            units = []
            for u in range(A_UNITS):
                idx = i * A_UNITS + u
                units.append((idx // (nb - 1), pl.multiple_of((idx % (nb - 1) + 1) * BLK, BLK), False))
            run(pi, refs, units, finish)
            return c

        if n_res >= A_UNITS:
            lax.fori_loop(0, n_res // A_UNITS, first_trip, 0)
            lax.fori_loop(0, n_res * (nb - 1) // A_UNITS, trip, 0)
        else:
            run(pi, refs, [(0, 0, True)] + [(0, b * BLK, False) for b in range(1, A_UNITS)], finish)

            def trip1(i, c):
                run(pi, refs, [(0, pl.multiple_of((i * A_UNITS + u) * BLK, BLK), False) for u in range(A_UNITS)],
                    finish)
                return c
            lax.fori_loop(1, nb // A_UNITS, trip1, 0)

    def finish16(unit, res):
        r, row0, _ = unit
        rows = pl.ds(r // 4 + 4 * row0, BLK, stride=4)
        for c in range(3):
            st16_s[c, r % 4, rows, :] = res[c]
    blocks(2, (q16_s, k16_s, v16_s), 16, finish16)

    def finish4(unit, res):
        r, row0, _ = unit
        res = merge(tuple(st16_s[c, r, pl.ds(row0, BLK), :] for c in range(3)), res)
        rows = pl.ds(r + 4 * row0, BLK, stride=4)
        for c in range(3):
            st4_s[c, rows, :] = res[c]
    blocks(1, (q4_s, k4_s, v4_s), 4, finish4)

    def finish1(unit, res):
        _, row0, _ = unit
        rows = pl.ds(row0, BLK)
        o, l, _ = merge(tuple(st4_s[c, rows, :] for c in range(3)), res)
        y_ref[rows, :] = (o / l * _silu(g_ref[rows, :])).astype(BF16)
    blocks(0, (q1_s, k1_s, v1_s), 1, finish1)


def _attn_a(z, batch, seq):
    assert DILATIONS == ((N_KEYS, 1), (4 * N_KEYS, 4), (16 * N_KEYS, 16)) and N_KEYS == BLK
    assert seq % (16 * BLK) == 0 and seq // 16 >= 2 * BLK and 4 % A_UNITS == 0 and seq // BLK % A_UNITS == 0
    spec = lambda off: pl.BlockSpec((None, seq, LANES), lambda b, hp: (off // LANES + hp, b, 0))
    res = lambda d, dt: pltpu.VMEM((d, seq // d, LANES), dt)
    return pl.pallas_call(
        functools.partial(_attn_a_kernel, seq=seq),
        grid=(batch, A_CB),
        in_specs=[spec(QA0), spec(KA0), spec(VA0), spec(GA0)],
        out_specs=pl.BlockSpec((None, seq, LANES), lambda b, hp: (hp, b, 0)),
        out_shape=jax.ShapeDtypeStruct((A_CB, batch * seq, LANES), BF16),
        scratch_shapes=[res(d, BF16) for d in (1, 1, 1, 4, 4, 4, 16, 16, 16)]
                       + [pltpu.VMEM((3, 4, seq // 4, LANES), F32), pltpu.VMEM((3, seq, LANES), F32),
                          pltpu.VMEM((3, 2 * BLK, 2 * BLK), F32), pltpu.VMEM((2, 2 * BLK, LANES), BF16)],
        compiler_params=_cparams(("parallel", "parallel")),
        name="attn_a",
    )(z, z, z, z)


def _attn_b_kernel(sink_ref, q_ref, kc_ref, kp_ref, vc_ref, vp_ref, g_ref, y_ref, kk_s, vv_s, bias_s):
    kvc = pl.program_id(1)
    ch = pl.program_id(2)
    lo = lax.broadcasted_iota(jnp.int32, (BLK, LANES), 1) < HEAD_DIM
    heads = 2 * Q_PER_KV

    for cur, prev, dst in ((kc_ref, kp_ref, kk_s), (vc_ref, vp_ref, vv_s)):
        for src, a, n in ((prev, 0, BLK), (cur, BLK, B_CHUNK)):
            x = src[...]
            dst[0, a:a + n, :] = x.astype(BF16)
            dst[1, a:a + n, :] = pltpu.roll(x, HEAD_DIM, 1).astype(BF16)

    in_prev = lax.broadcasted_iota(jnp.int32, (BLK, 2 * BLK), 1) < BLK
    for hl in range(heads):
        full = _band_bias(_head_slope(kvc * heads + hl, B_HEADS), 1)
        bias_s[0, hl] = full
        bias_s[1, hl] = jnp.where(in_prev, NEG, full)

    def body(i, c):
        row0 = pl.multiple_of(i * BLK, BLK)
        first = ((ch == 0) & (i == 0)).astype(jnp.int32)
        for qc in range(heads // 2):
            q = q_ref[qc, pl.ds(row0, BLK), :].astype(BF16)
            outs = []
            for half in range(2):
                hl = 2 * qc + half
                var = 0 if hl // Q_PER_KV == half else 1
                qm = jnp.where(lo if half == 0 else ~lo, q, jnp.zeros_like(q))
                kk = kk_s[var, pl.ds(row0, 2 * BLK), :]
                vv = vv_s[var, pl.ds(row0, 2 * BLK), :]
                sink = sink_ref[kvc * heads + hl]
                outs.append(_softmax_block(qm, kk, vv, bias_s[first, hl], sink=sink)[0])
            o = jnp.where(lo, outs[0], outs[1])
            y_ref[qc, pl.ds(row0, BLK), :] = (o * _silu(g_ref[qc, pl.ds(row0, BLK), :])).astype(BF16)
        return c
    lax.fori_loop(0, B_CHUNK // BLK, body, 0)


def _attn_b(z, sinks, batch, seq):
    assert seq % B_CHUNK == 0 and B_WINDOW == BLK
    nch = seq // B_CHUNK
    qpk = B_CB // KV_CB
    row = lambda b, c: b * nch + c
    prev = lambda b, c: jnp.maximum((b * nch + c) * (B_CHUNK // BLK) - 1, 0)
    cur_spec = lambda off: pl.BlockSpec((None, B_CHUNK, LANES),
                                        lambda b, kvc, c: (off // LANES + kvc, row(b, c), 0))
    prev_spec = lambda off: pl.BlockSpec((None, BLK, LANES),
                                         lambda b, kvc, c: (off // LANES + kvc, prev(b, c), 0))
    wide = lambda off: pl.BlockSpec((qpk, B_CHUNK, LANES),
                                    lambda b, kvc, c: (off // LANES // qpk + kvc, row(b, c), 0))
    return pl.pallas_call(
        _attn_b_kernel,
        grid=(batch, KV_CB, nch),
        in_specs=[pl.BlockSpec(memory_space=pltpu.SMEM),
                  wide(QB0), cur_spec(KB0), prev_spec(KB0), cur_spec(VB0), prev_spec(VB0), wide(GB0)],
        out_specs=pl.BlockSpec((qpk, B_CHUNK, LANES), lambda b, kvc, c: (kvc, row(b, c), 0)),
        out_shape=jax.ShapeDtypeStruct((B_CB, batch * seq, LANES), BF16),
        scratch_shapes=[pltpu.VMEM((2, BLK + B_CHUNK, LANES), BF16),
                        pltpu.VMEM((2, BLK + B_CHUNK, LANES), BF16),
                        pltpu.VMEM((2, 2 * Q_PER_KV, BLK, 2 * BLK), F32)],
        compiler_params=_cparams(("parallel", "parallel", "arbitrary")),
        name="attn_b",
    )(sinks, z, z, z, z, z, z)


def _decode_kernel(zt_ref, ka_ref, va_ref, kb_ref, vb_ref, bias_a_ref, bias_b_ref, sink_ref, sel_ref, ones_ref,
                   ya_ref, yb_ref):
    n_h = A_HEADS
    zt = zt_ref[...]
    cols = lambda off, n: zt[:, off // HEAD_DIM:off // HEAD_DIM + n]
    eye = lax.broadcasted_iota(jnp.int32, (n_h, n_h), 0) == lax.broadcasted_iota(jnp.int32, (n_h, n_h), 1)
    to_col = lambda row: jnp.sum(jnp.where(eye, jnp.broadcast_to(row, (n_h, n_h)), 0.0), axis=1, keepdims=True)
    to_row = lambda c: jnp.sum(jnp.where(eye, jnp.broadcast_to(c, (n_h, n_h)), 0.0), axis=0, keepdims=True)

    def mixer(q, k_ref, v_ref, kv_head, bias, k_new, v_new, self_bias, sink):
        t = k_ref.shape[-1]
        parts = [jnp.sum((k_ref[kv_head(h)] * q[:, h:h + 1]).reshape(HEAD_DIM // SUBLANES, SUBLANES, t), axis=0)
                 for h in range(n_h)]
        s = jnp.dot(sel_ref[...], jnp.concatenate(parts, axis=0).astype(BF16),
                    preferred_element_type=F32) + bias
        s_self = jnp.sum(q * k_new, axis=0, keepdims=True) + self_bias
        m = jnp.maximum(to_row(jnp.max(s, axis=1, keepdims=True)), s_self)
        if sink is not None:
            m = jnp.maximum(m, sink)
        p = jnp.exp(s - to_col(m))
        p_self = jnp.exp(s_self - m)
        den = to_row(jnp.sum(p, axis=1, keepdims=True)) + p_self
        if sink is not None:
            den = den + jnp.exp(sink - m)
        sums = []
        for h in range(n_h):
            w = v_ref[kv_head(h)] * p[h:h + 1, :]
            acc = w[:, :LANES]
            for j in range(1, t // LANES):
                acc = acc + w[:, j * LANES:(j + 1) * LANES]
            sums.append(acc)
        o = jnp.dot(jnp.concatenate(sums, axis=1).astype(BF16), ones_ref[...], preferred_element_type=F32)
        return (o + v_new * p_self) / den

    oa = mixer(cols(QA0, A_HEADS), ka_ref, va_ref, lambda h: h, bias_a_ref[...], cols(KA0, A_HEADS),
               cols(VA0, A_HEADS), math.log(len(DILATIONS)), None)
    ya_ref[...] = (oa * _silu(cols(GA0, A_HEADS))).astype(BF16)

    kv_of_lane = lax.broadcasted_iota(jnp.int32, (HEAD_DIM, B_HEADS), 1) // Q_PER_KV
    per_query = lambda off: sum(jnp.where(kv_of_lane == g, zt[:, off // HEAD_DIM + g:off // HEAD_DIM + g + 1], 0.0)
                                for g in range(B_KV_HEADS))
    ob = mixer(cols(QB0, B_HEADS), kb_ref, vb_ref, lambda h: h // Q_PER_KV, bias_b_ref[...], per_query(KB0),
               per_query(VB0), 0.0, sink_ref[...])
    yb_ref[...] = (ob * _silu(cols(GB0, B_HEADS))).astype(BF16)


def _decode(z_s, cache_a_k, cache_a_v, cache_b_k, cache_b_v, sinks):
    n = z_s.shape[0]
    assert cache_a_k.shape[1] == A_REACH and cache_b_k.shape[1] == B_WINDOW == N_KEYS
    slopes = lambda nh: jnp.exp2(-8.0 * jnp.arange(1, nh + 1, dtype=F32) / nh)
    dd = A_REACH - jnp.arange(A_REACH)
    mult = sum(((dd % d == 0) & (dd <= w)).astype(F32) for w, d in DILATIONS)
    bias_a = jnp.where(mult > 0, -slopes(A_HEADS)[:, None] * dd.astype(F32) + jnp.log(jnp.maximum(mult, 1.0)), NEG)
    bias_b = -slopes(B_HEADS)[:, None] * (B_WINDOW - jnp.arange(B_WINDOW)).astype(F32)
    by_head = lambda c: c.transpose(0, 2, 3, 1)
    buf_spec = lambda heads, rows: pl.BlockSpec((None, heads, HEAD_DIM, rows), lambda i: (i, 0, 0, 0))
    const = lambda a: pl.BlockSpec(a.shape, lambda i: (0,) * a.ndim)
    sink2 = sinks.reshape(1, B_HEADS)
    zt = z_s.reshape(n, D_IN // HEAD_DIM, HEAD_DIM).transpose(0, 2, 1)
    sel = (jnp.arange(A_HEADS * SUBLANES)[None, :] // SUBLANES == jnp.arange(A_HEADS)[:, None]).astype(BF16)
    ones = (jnp.arange(A_HEADS * LANES)[:, None] // LANES == jnp.arange(A_HEADS)[None, :]).astype(BF16)
    out_spec = pl.BlockSpec((None, HEAD_DIM, A_HEADS), lambda i: (i, 0, 0))
    yat, ybt = pl.pallas_call(
        _decode_kernel,
        grid=(n,),
        in_specs=[pl.BlockSpec((None, HEAD_DIM, D_IN // HEAD_DIM), lambda i: (i, 0, 0)),
                  buf_spec(A_HEADS, A_REACH), buf_spec(A_HEADS, A_REACH),
                  buf_spec(B_KV_HEADS, B_WINDOW), buf_spec(B_KV_HEADS, B_WINDOW),
                  const(bias_a), const(bias_b), const(sink2), const(sel), const(ones)],
        out_specs=[out_spec, out_spec],
        out_shape=[jax.ShapeDtypeStruct((n, HEAD_DIM, A_HEADS), BF16)] * 2,
        compiler_params=_cparams(("parallel",)),
        name="decode",
    )(zt, by_head(cache_a_k), by_head(cache_a_v), by_head(cache_b_k), by_head(cache_b_v), bias_a, bias_b, sink2,
      sel, ones)
    return tuple(yt.transpose(0, 2, 1).reshape(n, A_WIDTH) for yt in (yat, ybt))


def _out_kernel(x_ref, ya_ref, yb_ref, p_ref, wout_ref, gple_ref, wpg_ref, wple_ref, gfin_ref, o_ref):
    y = jnp.concatenate([ya_ref[c] for c in range(A_CB)] + [yb_ref[c] for c in range(B_CB)], axis=1)
    h = x_ref[...] + jnp.dot(y, wout_ref[...], preferred_element_type=F32)
    gate = jax.nn.sigmoid(jnp.dot(_rms(h, gple_ref[...]).astype(BF16), wpg_ref[...],
                                  preferred_element_type=F32))
    ple = jnp.dot(p_ref[...].astype(BF16), wple_ref[...], preferred_element_type=F32)
    o_ref[...] = _rms(h + ple * gate, gfin_ref[...])


def _out(x, ya, yb, p, wout_bf, g_ple, wpg_bf, wple_bf, g_final, tm):
    rows, d = x.shape
    resident = lambda a: pl.BlockSpec(a.shape, lambda i: (0,) * a.ndim, pipeline_mode=pl.Buffered(1))
    g_ple, g_final = g_ple.reshape(1, d), g_final.reshape(1, d)
    return pl.pallas_call(
        _out_kernel,
        grid=(rows // tm,),
        in_specs=[pl.BlockSpec((tm, d), lambda i: (i, 0)),
                  pl.BlockSpec((A_CB, tm, LANES), lambda i: (0, i, 0)),
                  pl.BlockSpec((B_CB, tm, LANES), lambda i: (0, i, 0)),
                  pl.BlockSpec((tm, p.shape[1]), lambda i: (i, 0)),
                  resident(wout_bf), resident(g_ple), resident(wpg_bf), resident(wple_bf), resident(g_final)],
        out_specs=pl.BlockSpec((tm, d), lambda i: (i, 0)),
        out_shape=jax.ShapeDtypeStruct((rows, d), F32),
        compiler_params=_cparams(("parallel",)),
        name="out",
    )(x, ya, yb, p, wout_bf, g_ple, wpg_bf, wple_bf, g_final)


def _rows_to_heads(z, off, width, batch, seq, keep):
    cb = z[off // LANES:(off + width) // LANES].reshape(width // LANES, batch, seq, LANES)[:, :, seq - keep:]
    return cb.transpose(1, 2, 0, 3).reshape(batch, keep, width // HEAD_DIM, HEAD_DIM)


def _col_blocks(y):
    return y.reshape(y.shape[0], y.shape[1] // LANES, LANES).transpose(1, 0, 2)


def kernel(x_prompt, x_sample, cache_a_k, cache_a_v, cache_b_k, cache_b_v, p_prompt, p_sample,
           g_mix, w_in, sinks, w_out, g_ple, w_pg, w_ple, g_final):
    depth = w_in.shape[0]
    batch, seq, d = x_prompt.shape
    n_s, t_s, _ = x_sample.shape
    assert depth == 1 and t_s == 1 and w_in.shape[2] == D_IN
    xp = x_prompt.reshape(batch * seq, d)
    xs = x_sample.reshape(n_s, d)
    i = 0
    w_in_bf, w_out_bf, w_pg_bf, w_ple_bf = (w[i].astype(BF16) for w in (w_in, w_out, w_pg, w_ple))

    z = _proj(xp, g_mix[i], w_in_bf, tm=1024)
    ya = _attn_a(z, batch, seq)
    yb = _attn_b(z, sinks[i], batch, seq)
    y_prompt = _out(xp, ya, yb, p_prompt[i].reshape(batch * seq, -1), w_out_bf, g_ple[i], w_pg_bf, w_ple_bf,
                    g_final, tm=256)
    new_p = [_rows_to_heads(z, off, width, batch, seq, keep)[None]
             for off, width, keep in ((KA0, A_WIDTH, min(A_REACH, seq)), (VA0, A_WIDTH, min(A_REACH, seq)),
                                      (KB0, B_KV_WIDTH, min(B_WINDOW, seq)), (VB0, B_KV_WIDTH, min(B_WINDOW, seq)))]

    z_s = _proj(xs, g_mix[i], w_in_bf, tm=n_s).transpose(1, 0, 2).reshape(n_s, D_IN)
    ya_s, yb_s = _decode(z_s, cache_a_k[i], cache_a_v[i], cache_b_k[i], cache_b_v[i], sinks[i])
    y_sample = _out(xs, _col_blocks(ya_s), _col_blocks(yb_s), p_sample[i].reshape(n_s, -1), w_out_bf, g_ple[i],
                    w_pg_bf, w_ple_bf, g_final, tm=n_s)
    new_s = [z_s[:, off:off + width].reshape(1, n_s, 1, width // HEAD_DIM, HEAD_DIM)
             for off, width in ((KA0, A_WIDTH), (VA0, A_WIDTH), (KB0, B_KV_WIDTH), (VB0, B_KV_WIDTH))]

    return (y_prompt.reshape(batch, seq, d), y_sample.reshape(n_s, t_s, d), *new_p, *new_s)
```

```python
import functools
import math

import jax
import jax.numpy as jnp
from jax import lax
from jax.experimental import pallas as pl
from jax.experimental.pallas import tpu as pltpu

F32 = jnp.float32
BF16 = jnp.bfloat16

LANES = 128
SUBLANES = 8
HEAD_DIM = 64
A_HEADS = 16
B_HEADS = 16
B_KV_HEADS = 4
BLK = 128
N_KEYS = 128
DILATIONS = ((128, 1), (512, 4), (2048, 16))
A_REACH = 2048
B_WINDOW = 128
EPS = 1e-6
NEG = -1e30

A_WIDTH = A_HEADS * HEAD_DIM
B_WIDTH = B_HEADS * HEAD_DIM
B_KV_WIDTH = B_KV_HEADS * HEAD_DIM
IN_SPLITS = (A_WIDTH, A_WIDTH, A_WIDTH, A_WIDTH, B_WIDTH, B_KV_WIDTH, B_KV_WIDTH, B_WIDTH)
D_IN = sum(IN_SPLITS)
N_CB = D_IN // LANES
QA0, KA0, VA0, GA0, QB0, KB0, VB0, GB0 = (sum(IN_SPLITS[:i]) for i in range(8))
A_CB = A_WIDTH // LANES
B_CB = B_WIDTH // LANES
KV_CB = B_KV_WIDTH // LANES
Q_PER_KV = B_HEADS // B_KV_HEADS

PROJ_TN = 512
B_CHUNK = 1024
A_UNITS = 4
LOG2E = math.log2(math.e)
Q_SCALE = HEAD_DIM ** -0.5 * LOG2E
VMEM_LIMIT = 56 * 1024 * 1024


def _cparams(sem):
    return pltpu.CompilerParams(dimension_semantics=sem, vmem_limit_bytes=VMEM_LIMIT)


def _rms(x, g):
    return x * lax.rsqrt(jnp.mean(x * x, axis=-1, keepdims=True) + EPS) * g


def _silu(g):
    return g * jax.nn.sigmoid(g)


def _proj_kernel(x_ref, g_ref, w_ref, z_ref, hn_ref):
    j = pl.program_id(1)

    @pl.when(j == 0)
    def _():
        hn_ref[...] = _rms(x_ref[...], g_ref[...]).astype(BF16)

    acc = jnp.dot(hn_ref[...], w_ref[...].astype(BF16), preferred_element_type=F32)
    col = j * PROJ_TN
    is_q = ((col >= QA0) & (col < QA0 + A_WIDTH)) | ((col >= QB0) & (col < QB0 + B_WIDTH))
    acc = acc * jnp.where(is_q, Q_SCALE, 1.0)
    for c in range(PROJ_TN // LANES):
        z_ref[c] = acc[:, c * LANES:(c + 1) * LANES]


def _proj(x, g, w, tm):
    rows, d = x.shape
    return pl.pallas_call(
        _proj_kernel,
        grid=(rows // tm, D_IN // PROJ_TN),
        in_specs=[pl.BlockSpec((tm, d), lambda i, j: (i, 0)),
                  pl.BlockSpec((1, d), lambda i, j: (0, 0)),
                  pl.BlockSpec((d, PROJ_TN), lambda i, j: (0, j))],
        out_specs=pl.BlockSpec((PROJ_TN // LANES, tm, LANES), lambda i, j: (j, i, 0)),
        out_shape=jax.ShapeDtypeStruct((N_CB, rows, LANES), F32),
        scratch_shapes=[pltpu.VMEM((tm, d), BF16)],
        compiler_params=_cparams(("parallel", "arbitrary")),
        name="proj",
    )(x, g.reshape(1, d), w)


def _band_bias(slope, step):
    qi = lax.broadcasted_iota(jnp.int32, (BLK, 2 * BLK), 0)
    kj = lax.broadcasted_iota(jnp.int32, (BLK, 2 * BLK), 1)
    dist = qi + BLK - kj
    valid = (dist >= 0) & (dist <= N_KEYS)
    return jnp.where(valid, -slope * (dist * step).astype(F32) * LOG2E, NEG)


def _head_slope(h, n_heads):
    return jnp.exp2(jnp.full((1, 1), -8.0 / n_heads, F32) * (h + 1).astype(F32))


def _softmax_block(q, kk, vv, bias, sink=None):
    s = lax.dot_general(q, kk, (((1,), (1,)), ((), ())), preferred_element_type=F32) + bias
    m = jnp.max(s, axis=1, keepdims=True)
    if sink is not None:
        m = jnp.maximum(m, sink)
    p = jnp.exp2(s - m)
    den = jnp.sum(p, axis=1, keepdims=True)
    if sink is not None:
        den = den + jnp.exp2(sink - m)
    return jnp.dot(p.astype(BF16), vv, preferred_element_type=F32) / den


def _attn_a_kernel(q_ref, k_ref, v_ref, g_ref, y_ref, kt_ref, vt_ref,
                   q1_s, k1_s, v1_s, q4_s, k4_s, v4_s, q16_s, k16_s, v16_s,
                   st16_s, st4_s, bias_s, ones_s, *, seq):
    keep = kt_ref.shape[-1]
    for src, dst in ((k_ref, kt_ref), (v_ref, vt_ref)):
        for j in range(keep // LANES):
            tile = src[seq - keep + j * LANES:seq - keep + (j + 1) * LANES, :]
            dst[:, :, j * LANES:(j + 1) * LANES] = tile.T.reshape(2, HEAD_DIM, LANES)

    hp = pl.program_id(1)
    lo = lax.broadcasted_iota(jnp.int32, (BLK, LANES), 1) < HEAD_DIM
    low_lanes = lambda rows: lax.broadcasted_iota(jnp.int32, (rows, LANES), 1) < HEAD_DIM
    steps = tuple(d for _, d in DILATIONS)

    for pi, step in enumerate(steps):
        for hh in range(2):
            bias_s[pi, hh * BLK:(hh + 1) * BLK, :] = _band_bias(_head_slope(2 * hp + hh, A_HEADS), step)
    ones_s[0] = jnp.where(low_lanes(2 * BLK), 1.0, 0.0).astype(BF16)
    ones_s[1] = jnp.where(low_lanes(2 * BLK), 0.0, 1.0).astype(BF16)

    for src, dsts in ((q_ref, (q1_s, q4_s, q16_s)), (k_ref, (k1_s, k4_s, k16_s)), (v_ref, (v1_s, v4_s, v16_s))):
        dsts[0][0] = src[...].astype(BF16)
        for dst, d in zip(dsts[1:], steps[1:]):
            for r in range(d):
                dst[r] = src[pl.ds(r, seq // d, stride=d), :].astype(BF16)

    def scores(pi, refs, r, row0, first):
        q_s, k_s, v_s = refs
        q = q_s[r, pl.ds(row0, BLK), :]
        q2 = jnp.concatenate([jnp.where(lo, q, jnp.zeros_like(q)), jnp.where(lo, jnp.zeros_like(q), q)], axis=0)
        if first:
            keys, bias = pl.ds(row0, BLK), bias_s[pi, :, BLK:]
        else:
            keys, bias = pl.ds(row0 - BLK, 2 * BLK), bias_s[pi]
        s = lax.dot_general(q2, k_s[r, keys, :], (((1,), (1,)), ((), ())), preferred_element_type=F32) + bias
        return s, v_s[r, keys, :]

    def attend(s, vv):
        nk = vv.shape[0]
        halves = (s[:BLK], s[BLK:])
        m = [jnp.max(h, axis=1, keepdims=True) for h in halves]
        lhs = jnp.concatenate([jnp.exp2(h - mh).astype(BF16) for h, mh in zip(halves, m)], axis=1)
        zero = jnp.zeros_like(vv)
        rhs = jnp.concatenate(
            [jnp.concatenate([jnp.where(low_lanes(nk), vv, zero), ones_s[0, :nk, :]], axis=1),
             jnp.concatenate([jnp.where(low_lanes(nk), zero, vv), ones_s[1, :nk, :]], axis=1)], axis=0)
        ol = jnp.dot(lhs, rhs, preferred_element_type=F32)
        mt = jnp.where(lo, jnp.broadcast_to(m[0], (BLK, LANES)), jnp.broadcast_to(m[1], (BLK, LANES)))
        return ol[:, :LANES], ol[:, LANES:], mt

    def merge(a, b):
        m = jnp.maximum(a[2], b[2])
        wa = jnp.exp2(a[2] - m)
        wb = jnp.exp2(b[2] - m)
        return wa * a[0] + wb * b[0], wa * a[1] + wb * b[1], m

    def run(pi, refs, units, finish):
        results = []
        nxt = scores(pi, refs, *units[0])
        for k, unit in enumerate(units):
            s, vv = nxt
            if k + 1 < len(units):
                nxt = scores(pi, refs, *units[k + 1])
            results.append(attend(s, vv))
        for unit, res in zip(units, results):
            finish(unit, res)

    def blocks(pi, refs, n_res, finish):
        nb = seq // steps[pi] // BLK

        def first_trip(i, c):
            run(pi, refs, [(i * A_UNITS + u, 0, True) for u in range(A_UNITS)], finish)
            return c

        def trip(i, c):
            units = []
            for u in range(A_UNITS):
                idx = i * A_UNITS + u
                units.append((idx // (nb - 1), pl.multiple_of((idx % (nb - 1) + 1) * BLK, BLK), False))
            run(pi, refs, units, finish)
            return c

        if n_res >= A_UNITS:
            lax.fori_loop(0, n_res // A_UNITS, first_trip, 0)
            lax.fori_loop(0, n_res * (nb - 1) // A_UNITS, trip, 0)
        else:
            run(pi, refs, [(0, 0, True)] + [(0, b * BLK, False) for b in range(1, A_UNITS)], finish)

            def trip1(i, c):
                run(pi, refs, [(0, pl.multiple_of((i * A_UNITS + u) * BLK, BLK), False) for u in range(A_UNITS)],
                    finish)
                return c
            lax.fori_loop(1, nb // A_UNITS, trip1, 0)

    def finish16(unit, res):
        r, row0, _ = unit
        rows = pl.ds(r // 4 + 4 * row0, BLK, stride=4)
        for c in range(3):
            st16_s[c, r % 4, rows, :] = res[c]
    blocks(2, (q16_s, k16_s, v16_s), 16, finish16)

    def finish4(unit, res):
        r, row0, _ = unit
        res = merge(tuple(st16_s[c, r, pl.ds(row0, BLK), :] for c in range(3)), res)
        rows = pl.ds(r + 4 * row0, BLK, stride=4)
        for c in range(3):
            st4_s[c, rows, :] = res[c]
    blocks(1, (q4_s, k4_s, v4_s), 4, finish4)

    def finish1(unit, res):
        _, row0, _ = unit
        rows = pl.ds(row0, BLK)
        o, l, _ = merge(tuple(st4_s[c, rows, :] for c in range(3)), res)
        y_ref[rows, :] = (o / l * _silu(g_ref[rows, :])).astype(BF16)
    blocks(0, (q1_s, k1_s, v1_s), 1, finish1)


def _attn_a(z, batch, seq):
    assert DILATIONS == ((N_KEYS, 1), (4 * N_KEYS, 4), (16 * N_KEYS, 16)) and N_KEYS == BLK
    assert seq % (16 * BLK) == 0 and seq // 16 >= 2 * BLK and 4 % A_UNITS == 0 and seq // BLK % A_UNITS == 0
    keep = min(A_REACH, seq)
    spec = lambda off: pl.BlockSpec((None, seq, LANES), lambda b, hp: (off // LANES + hp, b, 0))
    res = lambda d, dt: pltpu.VMEM((d, seq // d, LANES), dt)
    kv_spec = pl.BlockSpec((None, 2, HEAD_DIM, keep), lambda b, hp: (b, hp, 0, 0))
    kv_shape = jax.ShapeDtypeStruct((batch, A_HEADS, HEAD_DIM, keep), F32)
    return pl.pallas_call(
        functools.partial(_attn_a_kernel, seq=seq),
        grid=(batch, A_CB),
        in_specs=[spec(QA0), spec(KA0), spec(VA0), spec(GA0)],
        out_specs=[pl.BlockSpec((None, seq, LANES), lambda b, hp: (hp, b, 0)), kv_spec, kv_spec],
        out_shape=[jax.ShapeDtypeStruct((A_CB, batch * seq, LANES), BF16), kv_shape, kv_shape],
        scratch_shapes=[res(d, BF16) for d in (1, 1, 1, 4, 4, 4, 16, 16, 16)]
                       + [pltpu.VMEM((3, 4, seq // 4, LANES), F32), pltpu.VMEM((3, seq, LANES), F32),
                          pltpu.VMEM((3, 2 * BLK, 2 * BLK), F32), pltpu.VMEM((2, 2 * BLK, LANES), BF16)],
        compiler_params=_cparams(("parallel", "parallel")),
        name="attn_a",
    )(z, z, z, z)


def _attn_b_kernel(sink_ref, q_ref, kc_ref, kp_ref, vc_ref, vp_ref, g_ref, y_ref, kt_ref, vt_ref,
                   kk_s, vv_s, bias_s):
    kvc = pl.program_id(1)
    ch = pl.program_id(2)
    lo = lax.broadcasted_iota(jnp.int32, (BLK, LANES), 1) < HEAD_DIM
    heads = 2 * Q_PER_KV

    @pl.when(ch == pl.num_programs(2) - 1)
    def _():
        keep = kt_ref.shape[-1]
        for src, dst in ((kc_ref, kt_ref), (vc_ref, vt_ref)):
            dst[...] = src[B_CHUNK - keep:, :].T.reshape(2, HEAD_DIM, keep)

    for cur, prev, dst in ((kc_ref, kp_ref, kk_s), (vc_ref, vp_ref, vv_s)):
        for src, a, n in ((prev, 0, BLK), (cur, BLK, B_CHUNK)):
            x = src[...]
            dst[0, a:a + n, :] = x.astype(BF16)
            dst[1, a:a + n, :] = pltpu.roll(x, HEAD_DIM, 1).astype(BF16)

    in_prev = lax.broadcasted_iota(jnp.int32, (BLK, 2 * BLK), 1) < BLK
    for hl in range(heads):
        full = _band_bias(_head_slope(kvc * heads + hl, B_HEADS), 1)
        bias_s[0, hl] = full
        bias_s[1, hl] = jnp.where(in_prev, NEG, full)

    def body(i, c):
        row0 = pl.multiple_of(i * BLK, BLK)
        first = ((ch == 0) & (i == 0)).astype(jnp.int32)
        for qc in range(heads // 2):
            q = q_ref[qc, pl.ds(row0, BLK), :].astype(BF16)
            outs = []
            for half in range(2):
                hl = 2 * qc + half
                var = 0 if hl // Q_PER_KV == half else 1
                qm = jnp.where(lo if half == 0 else ~lo, q, jnp.zeros_like(q))
                kk = kk_s[var, pl.ds(row0, 2 * BLK), :]
                vv = vv_s[var, pl.ds(row0, 2 * BLK), :]
                sink = sink_ref[kvc * heads + hl] * LOG2E
                outs.append(_softmax_block(qm, kk, vv, bias_s[first, hl], sink=sink))
            o = jnp.where(lo, outs[0], outs[1])
            y_ref[qc, pl.ds(row0, BLK), :] = (o * _silu(g_ref[qc, pl.ds(row0, BLK), :])).astype(BF16)
        return c
    lax.fori_loop(0, B_CHUNK // BLK, body, 0)


def _attn_b(z, sinks, batch, seq):
    assert seq % B_CHUNK == 0 and B_WINDOW == BLK
    nch = seq // B_CHUNK
    keep = min(B_WINDOW, seq)
    kv_spec = pl.BlockSpec((None, 2, HEAD_DIM, keep), lambda b, kvc, c: (b, kvc, 0, 0))
    kv_shape = jax.ShapeDtypeStruct((batch, B_KV_HEADS, HEAD_DIM, keep), F32)
    qpk = B_CB // KV_CB
    row = lambda b, c: b * nch + c
    prev = lambda b, c: jnp.maximum((b * nch + c) * (B_CHUNK // BLK) - 1, 0)
    cur_spec = lambda off: pl.BlockSpec((None, B_CHUNK, LANES),
                                        lambda b, kvc, c: (off // LANES + kvc, row(b, c), 0))
    prev_spec = lambda off: pl.BlockSpec((None, BLK, LANES),
                                         lambda b, kvc, c: (off // LANES + kvc, prev(b, c), 0))
    wide = lambda off: pl.BlockSpec((qpk, B_CHUNK, LANES),
                                    lambda b, kvc, c: (off // LANES // qpk + kvc, row(b, c), 0))
    return pl.pallas_call(
        _attn_b_kernel,
        grid=(batch, KV_CB, nch),
        in_specs=[pl.BlockSpec(memory_space=pltpu.SMEM),
                  wide(QB0), cur_spec(KB0), prev_spec(KB0), cur_spec(VB0), prev_spec(VB0), wide(GB0)],
        out_specs=[pl.BlockSpec((qpk, B_CHUNK, LANES), lambda b, kvc, c: (kvc, row(b, c), 0)), kv_spec, kv_spec],
        out_shape=[jax.ShapeDtypeStruct((B_CB, batch * seq, LANES), BF16), kv_shape, kv_shape],
        scratch_shapes=[pltpu.VMEM((2, BLK + B_CHUNK, LANES), BF16),
                        pltpu.VMEM((2, BLK + B_CHUNK, LANES), BF16),
                        pltpu.VMEM((2, 2 * Q_PER_KV, BLK, 2 * BLK), F32)],
        compiler_params=_cparams(("parallel", "parallel", "arbitrary")),
        name="attn_b",
    )(sinks, z, z, z, z, z, z)


def _decode_kernel(zt_ref, ka_ref, va_ref, kb_ref, vb_ref, bias_a_ref, bias_b_ref, sink_ref, sel_ref, ones_ref,
                   ya_ref, yb_ref):
    n_h = A_HEADS
    zt = zt_ref[...]
    cols = lambda off, n: zt[:, off // HEAD_DIM:off // HEAD_DIM + n]
    eye = lax.broadcasted_iota(jnp.int32, (n_h, n_h), 0) == lax.broadcasted_iota(jnp.int32, (n_h, n_h), 1)
    to_col = lambda row: jnp.sum(jnp.where(eye, jnp.broadcast_to(row, (n_h, n_h)), 0.0), axis=1, keepdims=True)
    to_row = lambda c: jnp.sum(jnp.where(eye, jnp.broadcast_to(c, (n_h, n_h)), 0.0), axis=0, keepdims=True)

    def mixer(q, k_ref, v_ref, kv_head, bias, k_new, v_new, self_bias, sink):
        t = k_ref.shape[-1]
        parts = [jnp.sum((k_ref[kv_head(h)] * q[:, h:h + 1]).reshape(HEAD_DIM // SUBLANES, SUBLANES, t), axis=0)
                 for h in range(n_h)]
        s = jnp.dot(sel_ref[...], jnp.concatenate(parts, axis=0).astype(BF16),
                    preferred_element_type=F32) + bias
        s_self = jnp.sum(q * k_new, axis=0, keepdims=True) + self_bias
        m = jnp.maximum(to_row(jnp.max(s, axis=1, keepdims=True)), s_self)
        if sink is not None:
            m = jnp.maximum(m, sink)
        p = jnp.exp2(s - to_col(m))
        p_self = jnp.exp2(s_self - m)
        den = to_row(jnp.sum(p, axis=1, keepdims=True)) + p_self
        if sink is not None:
            den = den + jnp.exp2(sink - m)
        sums = []
        for h in range(n_h):
            w = v_ref[kv_head(h)] * p[h:h + 1, :]
            acc = w[:, :LANES]
            for j in range(1, t // LANES):
                acc = acc + w[:, j * LANES:(j + 1) * LANES]
            sums.append(acc)
        o = jnp.dot(jnp.concatenate(sums, axis=1).astype(BF16), ones_ref[...], preferred_element_type=F32)
        return (o + v_new * p_self) / den

    oa = mixer(cols(QA0, A_HEADS), ka_ref, va_ref, lambda h: h, bias_a_ref[...], cols(KA0, A_HEADS),
               cols(VA0, A_HEADS), math.log2(len(DILATIONS)), None)
    ya_ref[...] = (oa * _silu(cols(GA0, A_HEADS))).astype(BF16)

    kv_of_lane = lax.broadcasted_iota(jnp.int32, (HEAD_DIM, B_HEADS), 1) // Q_PER_KV
    per_query = lambda off: sum(jnp.where(kv_of_lane == g, zt[:, off // HEAD_DIM + g:off // HEAD_DIM + g + 1], 0.0)
                                for g in range(B_KV_HEADS))
    ob = mixer(cols(QB0, B_HEADS), kb_ref, vb_ref, lambda h: h // Q_PER_KV, bias_b_ref[...], per_query(KB0),
               per_query(VB0), 0.0, sink_ref[...])
    yb_ref[...] = (ob * _silu(cols(GB0, B_HEADS))).astype(BF16)


def _decode(z_s, cache_a_k, cache_a_v, cache_b_k, cache_b_v, sinks):
    n = z_s.shape[0]
    assert cache_a_k.shape[1] == A_REACH and cache_b_k.shape[1] == B_WINDOW == N_KEYS
    slopes = lambda nh: jnp.exp2(-8.0 * jnp.arange(1, nh + 1, dtype=F32) / nh)
    dd = A_REACH - jnp.arange(A_REACH)
    mult = sum(((dd % d == 0) & (dd <= w)).astype(F32) for w, d in DILATIONS)
    bias_a = jnp.where(mult > 0, -slopes(A_HEADS)[:, None] * dd.astype(F32) * LOG2E
                       + jnp.log2(jnp.maximum(mult, 1.0)), NEG)
    bias_b = -slopes(B_HEADS)[:, None] * (B_WINDOW - jnp.arange(B_WINDOW)).astype(F32) * LOG2E
    by_head = lambda c: c.transpose(0, 2, 3, 1)
    buf_spec = lambda heads, rows: pl.BlockSpec((None, heads, HEAD_DIM, rows), lambda i: (i, 0, 0, 0))
    const = lambda a: pl.BlockSpec(a.shape, lambda i: (0,) * a.ndim)
    sink2 = sinks.reshape(1, B_HEADS) * LOG2E
    zt = z_s.reshape(n, D_IN // HEAD_DIM, HEAD_DIM).transpose(0, 2, 1)
    sel = (jnp.arange(A_HEADS * SUBLANES)[None, :] // SUBLANES == jnp.arange(A_HEADS)[:, None]).astype(BF16)
    ones = (jnp.arange(A_HEADS * LANES)[:, None] // LANES == jnp.arange(A_HEADS)[None, :]).astype(BF16)
    out_spec = pl.BlockSpec((None, HEAD_DIM, A_HEADS), lambda i: (i, 0, 0))
    yat, ybt = pl.pallas_call(
        _decode_kernel,
        grid=(n,),
        in_specs=[pl.BlockSpec((None, HEAD_DIM, D_IN // HEAD_DIM), lambda i: (i, 0, 0)),
                  buf_spec(A_HEADS, A_REACH), buf_spec(A_HEADS, A_REACH),
                  buf_spec(B_KV_HEADS, B_WINDOW), buf_spec(B_KV_HEADS, B_WINDOW),
                  const(bias_a), const(bias_b), const(sink2), const(sel), const(ones)],
        out_specs=[out_spec, out_spec],
        out_shape=[jax.ShapeDtypeStruct((n, HEAD_DIM, A_HEADS), BF16)] * 2,
        compiler_params=_cparams(("parallel",)),
        name="decode",
    )(zt, by_head(cache_a_k), by_head(cache_a_v), by_head(cache_b_k), by_head(cache_b_v), bias_a, bias_b, sink2,
      sel, ones)
    return tuple(yt.transpose(0, 2, 1).reshape(n, A_WIDTH) for yt in (yat, ybt))


def _out_kernel(x_ref, ya_ref, yb_ref, p_ref, wout_ref, gple_ref, wpg_ref, wple_ref, gfin_ref, o_ref):
    y = jnp.concatenate([ya_ref[c] for c in range(A_CB)] + [yb_ref[c] for c in range(B_CB)], axis=1)
    h = x_ref[...] + jnp.dot(y, wout_ref[...], preferred_element_type=F32)
    gate = jax.nn.sigmoid(jnp.dot(_rms(h, gple_ref[...]).astype(BF16), wpg_ref[...],
                                  preferred_element_type=F32))
    ple = jnp.dot(p_ref[...].astype(BF16), wple_ref[...], preferred_element_type=F32)
    o_ref[...] = _rms(h + ple * gate, gfin_ref[...])


def _out(x, ya, yb, p, wout_bf, g_ple, wpg_bf, wple_bf, g_final, tm):
    rows, d = x.shape
    resident = lambda a: pl.BlockSpec(a.shape, lambda i: (0,) * a.ndim, pipeline_mode=pl.Buffered(1))
    g_ple, g_final = g_ple.reshape(1, d), g_final.reshape(1, d)
    return pl.pallas_call(
        _out_kernel,
        grid=(rows // tm,),
        in_specs=[pl.BlockSpec((tm, d), lambda i: (i, 0)),
                  pl.BlockSpec((A_CB, tm, LANES), lambda i: (0, i, 0)),
                  pl.BlockSpec((B_CB, tm, LANES), lambda i: (0, i, 0)),
                  pl.BlockSpec((tm, p.shape[1]), lambda i: (i, 0)),
                  resident(wout_bf), resident(g_ple), resident(wpg_bf), resident(wple_bf), resident(g_final)],
        out_specs=pl.BlockSpec((tm, d), lambda i: (i, 0)),
        out_shape=jax.ShapeDtypeStruct((rows, d), F32),
        compiler_params=_cparams(("parallel",)),
        name="out",
    )(x, ya, yb, p, wout_bf, g_ple, wpg_bf, wple_bf, g_final)


def _col_blocks(y):
    return y.reshape(y.shape[0], y.shape[1] // LANES, LANES).transpose(1, 0, 2)


def kernel(x_prompt, x_sample, cache_a_k, cache_a_v, cache_b_k, cache_b_v, p_prompt, p_sample,
           g_mix, w_in, sinks, w_out, g_ple, w_pg, w_ple, g_final):
    depth = w_in.shape[0]
    batch, seq, d = x_prompt.shape
    n_s, t_s, _ = x_sample.shape
    assert depth == 1 and t_s == 1 and w_in.shape[2] == D_IN
    xp = x_prompt.reshape(batch * seq, d)
    xs = x_sample.reshape(n_s, d)
    i = 0
    w_out_bf, w_pg_bf, w_ple_bf = (w[i].astype(BF16) for w in (w_out, w_pg, w_ple))

    z = _proj(xp, g_mix[i], w_in[i], tm=1024)
    ya, ak_t, av_t = _attn_a(z, batch, seq)
    yb, bk_t, bv_t = _attn_b(z, sinks[i], batch, seq)
    y_prompt = _out(xp, ya, yb, p_prompt[i].reshape(batch * seq, -1), w_out_bf, g_ple[i], w_pg_bf, w_ple_bf,
                    g_final, tm=256)
    new_p = [t.transpose(0, 3, 1, 2)[None] for t in (ak_t, av_t, bk_t, bv_t)]

    z_s = _proj(xs, g_mix[i], w_in[i], tm=n_s).transpose(1, 0, 2).reshape(n_s, D_IN)
    ya_s, yb_s = _decode(z_s, cache_a_k[i], cache_a_v[i], cache_b_k[i], cache_b_v[i], sinks[i])
    y_sample = _out(xs, _col_blocks(ya_s), _col_blocks(yb_s), p_sample[i].reshape(n_s, -1), w_out_bf, g_ple[i],
                    w_pg_bf, w_ple_bf, g_final, tm=n_s)
    new_s = [z_s[:, off:off + width].reshape(1, n_s, 1, width // HEAD_DIM, HEAD_DIM)
             for off, width in ((KA0, A_WIDTH), (VA0, A_WIDTH), (KB0, B_KV_WIDTH), (VB0, B_KV_WIDTH))]

    return (y_prompt.reshape(batch, seq, d), y_sample.reshape(n_s, t_s, d), *new_p, *new_s)
```

```python
import functools
import math

import jax
import jax.numpy as jnp
from jax import lax
from jax.experimental import pallas as pl
from jax.experimental.pallas import tpu as pltpu

F32 = jnp.float32
BF16 = jnp.bfloat16

LANES = 128
SUBLANES = 8
HEAD_DIM = 64
A_HEADS = 16
B_HEADS = 16
B_KV_HEADS = 4
BLK = 128
N_KEYS = 128
DILATIONS = ((128, 1), (512, 4), (2048, 16))
A_REACH = 2048
B_WINDOW = 128
EPS = 1e-6
NEG = -1e30

A_WIDTH = A_HEADS * HEAD_DIM
B_WIDTH = B_HEADS * HEAD_DIM
B_KV_WIDTH = B_KV_HEADS * HEAD_DIM
IN_SPLITS = (A_WIDTH, A_WIDTH, A_WIDTH, A_WIDTH, B_WIDTH, B_KV_WIDTH, B_KV_WIDTH, B_WIDTH)
D_IN = sum(IN_SPLITS)
N_CB = D_IN // LANES
QA0, KA0, VA0, GA0, QB0, KB0, VB0, GB0 = (sum(IN_SPLITS[:i]) for i in range(8))
A_CB = A_WIDTH // LANES
B_CB = B_WIDTH // LANES
KV_CB = B_KV_WIDTH // LANES
Q_PER_KV = B_HEADS // B_KV_HEADS

PROJ_TN = 512
B_CHUNK = 1024
A_UNITS = 16
B_UNITS = 16
LOG2E = math.log2(math.e)
Q_SCALE = HEAD_DIM ** -0.5 * LOG2E
VMEM_LIMIT = 56 * 1024 * 1024


def _cparams(sem):
    return pltpu.CompilerParams(dimension_semantics=sem, vmem_limit_bytes=VMEM_LIMIT)


def _rms(x, g):
    return x * lax.rsqrt(jnp.mean(x * x, axis=-1, keepdims=True) + EPS) * g


def _silu(g):
    return g * jax.nn.sigmoid(g)


def _proj_kernel(x_ref, g_ref, w_ref, z_ref, hn_ref):
    j = pl.program_id(1)

    @pl.when(j == 0)
    def _():
        hn_ref[...] = _rms(x_ref[...], g_ref[...]).astype(BF16)

    acc = jnp.dot(hn_ref[...], w_ref[...].astype(BF16), preferred_element_type=F32)
    col = j * PROJ_TN
    is_q = ((col >= QA0) & (col < QA0 + A_WIDTH)) | ((col >= QB0) & (col < QB0 + B_WIDTH))
    acc = acc * jnp.where(is_q, Q_SCALE, 1.0)
    for c in range(PROJ_TN // LANES):
        z_ref[c] = acc[:, c * LANES:(c + 1) * LANES]


def _proj(x, g, w, tm):
    rows, d = x.shape
    return pl.pallas_call(
        _proj_kernel,
        grid=(rows // tm, D_IN // PROJ_TN),
        in_specs=[pl.BlockSpec((tm, d), lambda i, j: (i, 0)),
                  pl.BlockSpec((1, d), lambda i, j: (0, 0)),
                  pl.BlockSpec((d, PROJ_TN), lambda i, j: (0, j))],
        out_specs=pl.BlockSpec((PROJ_TN // LANES, tm, LANES), lambda i, j: (j, i, 0)),
        out_shape=jax.ShapeDtypeStruct((N_CB, rows, LANES), F32),
        scratch_shapes=[pltpu.VMEM((tm, d), BF16)],
        compiler_params=_cparams(("parallel", "arbitrary")),
        name="proj",
    )(x, g.reshape(1, d), w)


def _band_bias(slope, step):
    qi = lax.broadcasted_iota(jnp.int32, (BLK, 2 * BLK), 0)
    kj = lax.broadcasted_iota(jnp.int32, (BLK, 2 * BLK), 1)
    dist = qi + BLK - kj
    valid = (dist >= 0) & (dist <= N_KEYS)
    return jnp.where(valid, -slope * (dist * step).astype(F32) * LOG2E, NEG)


def _head_slope(h, n_heads):
    return jnp.exp2(jnp.full((1, 1), -8.0 / n_heads, F32) * (h + 1).astype(F32))


def _low_lanes(rows):
    return lax.broadcasted_iota(jnp.int32, (rows, LANES), 1) < HEAD_DIM


def _fill_ones(ones_s):
    ones_s[0] = jnp.where(_low_lanes(2 * BLK), 1.0, 0.0).astype(BF16)
    ones_s[1] = jnp.where(_low_lanes(2 * BLK), 0.0, 1.0).astype(BF16)


def _run_blocks(units, load, ones_s, finish, sinks_of=None):
    lo = _low_lanes(BLK)

    def scores(unit):
        q, kk, vv, bias = load(unit)
        zero = jnp.zeros_like(q)
        q2 = jnp.concatenate([jnp.where(lo, q, zero), jnp.where(lo, zero, q)], axis=0)
        return lax.dot_general(q2, kk, (((1,), (1,)), ((), ())), preferred_element_type=F32) + bias, vv

    def attend(s, vv, sinks):
        nk = vv.shape[0]
        halves = (s[:BLK], s[BLK:])
        m = [jnp.max(h, axis=1, keepdims=True) for h in halves]
        if sinks is not None:
            m = [jnp.maximum(mh, sk) for mh, sk in zip(m, sinks)]
        lhs = jnp.concatenate([jnp.exp2(h - mh).astype(BF16) for h, mh in zip(halves, m)], axis=1)
        zero = jnp.zeros_like(vv)
        rhs = jnp.concatenate(
            [jnp.concatenate([jnp.where(_low_lanes(nk), vv, zero), ones_s[0, :nk, :]], axis=1),
             jnp.concatenate([jnp.where(_low_lanes(nk), zero, vv), ones_s[1, :nk, :]], axis=1)], axis=0)
        ol = jnp.dot(lhs, rhs, preferred_element_type=F32)
        mt = jnp.where(lo, jnp.broadcast_to(m[0], (BLK, LANES)), jnp.broadcast_to(m[1], (BLK, LANES)))
        den = ol[:, LANES:]
        if sinks is not None:
            den = den + jnp.exp2(jnp.where(lo, sinks[0], sinks[1]) - mt)
        return ol[:, :LANES], den, mt

    results = []
    nxt = scores(units[0])
    for k, unit in enumerate(units):
        s, vv = nxt
        if k + 1 < len(units):
            nxt = scores(units[k + 1])
        results.append(attend(s, vv, None if sinks_of is None else sinks_of(unit)))
    for unit, res in zip(units, results):
        finish(unit, res)


def _attn_a_kernel(q_ref, k_ref, v_ref, g_ref, y_ref, kt_ref, vt_ref,
                   q1_s, k1_s, v1_s, q4_s, k4_s, v4_s, q16_s, k16_s, v16_s,
                   st16_s, st4_s, bias_s, ones_s, *, seq):
    keep = kt_ref.shape[-1]
    for src, dst in ((k_ref, kt_ref), (v_ref, vt_ref)):
        for j in range(keep // LANES):
            tile = src[seq - keep + j * LANES:seq - keep + (j + 1) * LANES, :]
            dst[:, :, j * LANES:(j + 1) * LANES] = tile.T.reshape(2, HEAD_DIM, LANES)

    hp = pl.program_id(1)
    steps = tuple(d for _, d in DILATIONS)

    for pi, step in enumerate(steps):
        for hh in range(2):
            bias_s[pi, hh * BLK:(hh + 1) * BLK, :] = _band_bias(_head_slope(2 * hp + hh, A_HEADS), step)
    _fill_ones(ones_s)

    for src, dsts in ((q_ref, (q1_s, q4_s, q16_s)), (k_ref, (k1_s, k4_s, k16_s)), (v_ref, (v1_s, v4_s, v16_s))):
        dsts[0][0] = src[...].astype(BF16)
        for dst, d in zip(dsts[1:], steps[1:]):
            for r in range(d):
                dst[r] = src[pl.ds(r, seq // d, stride=d), :].astype(BF16)

    def merge(a, b):
        m = jnp.maximum(a[2], b[2])
        wa = jnp.exp2(a[2] - m)
        wb = jnp.exp2(b[2] - m)
        return wa * a[0] + wb * b[0], wa * a[1] + wb * b[1], m

    def blocks(pi, refs, n_res, finish):
        nb = seq // steps[pi] // BLK
        q_s, k_s, v_s = refs

        def load(unit):
            r, row0, first = unit
            if first:
                keys, bias = pl.ds(row0, BLK), bias_s[pi, :, BLK:]
            else:
                keys, bias = pl.ds(row0 - BLK, 2 * BLK), bias_s[pi]
            return q_s[r, pl.ds(row0, BLK), :], k_s[r, keys, :], v_s[r, keys, :], bias

        def run(units):
            _run_blocks(units, load, ones_s, finish)

        def trips(n, unit_of):
            if n == 0:
                return
            per = max(u for u in range(1, A_UNITS + 1) if n % u == 0)
            if n == per:
                run([unit_of(i) for i in range(n)])
                return

            def trip(i, c):
                run([unit_of(i * per + u) for u in range(per)])
                return c
            lax.fori_loop(0, n // per, trip, 0)

        row = lambda b: b * BLK if isinstance(b, int) else pl.multiple_of(b * BLK, BLK)
        if n_res > 1:
            trips(n_res, lambda idx: (idx, 0, True))
            trips(n_res * (nb - 1), lambda idx: (idx // (nb - 1), row(idx % (nb - 1) + 1), False))
        else:
            run([(0, 0, True)] + [(0, row(b), False) for b in range(1, A_UNITS)])
            trips(nb - A_UNITS, lambda idx: (0, row(idx + A_UNITS), False))

    def finish16(unit, res):
        r, row0, _ = unit
        rows = pl.ds(r // 4 + 4 * row0, BLK, stride=4)
        for c in range(3):
            st16_s[c, r % 4, rows, :] = res[c]
    blocks(2, (q16_s, k16_s, v16_s), 16, finish16)

    def finish4(unit, res):
        r, row0, _ = unit
        res = merge(tuple(st16_s[c, r, pl.ds(row0, BLK), :] for c in range(3)), res)
        rows = pl.ds(r + 4 * row0, BLK, stride=4)
        for c in range(3):
            st4_s[c, rows, :] = res[c]
    blocks(1, (q4_s, k4_s, v4_s), 4, finish4)

    def finish1(unit, res):
        _, row0, _ = unit
        rows = pl.ds(row0, BLK)
        o, l, _ = merge(tuple(st4_s[c, rows, :] for c in range(3)), res)
        y_ref[rows, :] = (o / l * _silu(g_ref[rows, :])).astype(BF16)
    blocks(0, (q1_s, k1_s, v1_s), 1, finish1)


def _attn_a(z, batch, seq):
    assert DILATIONS == ((N_KEYS, 1), (4 * N_KEYS, 4), (16 * N_KEYS, 16)) and N_KEYS == BLK
    assert seq % (16 * BLK) == 0 and seq // 16 >= 2 * BLK and seq // BLK >= A_UNITS
    keep = min(A_REACH, seq)
    spec = lambda off: pl.BlockSpec((None, seq, LANES), lambda b, hp: (off // LANES + hp, b, 0))
    res = lambda d, dt: pltpu.VMEM((d, seq // d, LANES), dt)
    kv_spec = pl.BlockSpec((None, 2, HEAD_DIM, keep), lambda b, hp: (b, hp, 0, 0))
    kv_shape = jax.ShapeDtypeStruct((batch, A_HEADS, HEAD_DIM, keep), F32)
    return pl.pallas_call(
        functools.partial(_attn_a_kernel, seq=seq),
        grid=(batch, A_CB),
        in_specs=[spec(QA0), spec(KA0), spec(VA0), spec(GA0)],
        out_specs=[pl.BlockSpec((None, seq, LANES), lambda b, hp: (hp, b, 0)), kv_spec, kv_spec],
        out_shape=[jax.ShapeDtypeStruct((A_CB, batch * seq, LANES), BF16), kv_shape, kv_shape],
        scratch_shapes=[res(d, BF16) for d in (1, 1, 1, 4, 4, 4, 16, 16, 16)]
                       + [pltpu.VMEM((3, 4, seq // 4, LANES), F32), pltpu.VMEM((3, seq, LANES), F32),
                          pltpu.VMEM((3, 2 * BLK, 2 * BLK), F32), pltpu.VMEM((2, 2 * BLK, LANES), BF16)],
        compiler_params=_cparams(("parallel", "parallel")),
        name="attn_a",
    )(z, z, z, z)


def _attn_b_kernel(sink_ref, q_ref, kc_ref, kp_ref, vc_ref, vp_ref, g_ref, y_ref, kt_ref, vt_ref,
                   kk_s, vv_s, bias_s, ones_s):
    kvc = pl.program_id(1)
    ch = pl.program_id(2)
    heads = 2 * Q_PER_KV
    q_blocks = heads // 2

    @pl.when(ch == pl.num_programs(2) - 1)
    def _():
        keep = kt_ref.shape[-1]
        for src, dst in ((kc_ref, kt_ref), (vc_ref, vt_ref)):
            dst[...] = src[B_CHUNK - keep:, :].T.reshape(2, HEAD_DIM, keep)

    for cur, prev, dst in ((kc_ref, kp_ref, kk_s), (vc_ref, vp_ref, vv_s)):
        for src, a, n in ((prev, 0, BLK), (cur, BLK, B_CHUNK)):
            x = src[...]
            swapped = pltpu.roll(x, HEAD_DIM, 1)
            dst[0, a:a + n, :] = jnp.where(_low_lanes(n), x, swapped).astype(BF16)
            dst[1, a:a + n, :] = jnp.where(_low_lanes(n), swapped, x).astype(BF16)
    _fill_ones(ones_s)

    in_prev = lax.broadcasted_iota(jnp.int32, (BLK, 2 * BLK), 1) < BLK
    for hl in range(heads):
        full = _band_bias(_head_slope(kvc * heads + hl, B_HEADS), 1)
        rows = slice(hl % 2 * BLK, (hl % 2 + 1) * BLK)
        bias_s[0, hl // 2, rows, :] = full
        bias_s[1, hl // 2, rows, :] = jnp.where(in_prev, NEG, full)
    at_start = (ch == 0).astype(jnp.int32)

    def load(unit):
        i, qc = unit
        g = 2 * qc // Q_PER_KV
        keys = pl.ds(i * BLK, 2 * BLK)
        return (q_ref[qc, pl.ds(i * BLK, BLK), :].astype(BF16), kk_s[g, keys, :], vv_s[g, keys, :],
                bias_s[at_start if i == 0 else 0, qc])

    def sinks_of(unit):
        return tuple(sink_ref[kvc * heads + 2 * unit[1] + hh] * LOG2E for hh in range(2))

    def finish(unit, res):
        i, qc = unit
        rows = pl.ds(i * BLK, BLK)
        y_ref[qc, rows, :] = (res[0] / res[1] * _silu(g_ref[qc, rows, :])).astype(BF16)

    units = [(i, qc) for i in range(B_CHUNK // BLK) for qc in range(q_blocks)]
    for a in range(0, len(units), B_UNITS):
        _run_blocks(units[a:a + B_UNITS], load, ones_s, finish, sinks_of)


def _attn_b(z, sinks, batch, seq):
    assert seq % B_CHUNK == 0 and B_WINDOW == BLK
    nch = seq // B_CHUNK
    keep = min(B_WINDOW, seq)
    kv_spec = pl.BlockSpec((None, 2, HEAD_DIM, keep), lambda b, kvc, c: (b, kvc, 0, 0))
    kv_shape = jax.ShapeDtypeStruct((batch, B_KV_HEADS, HEAD_DIM, keep), F32)
    qpk = B_CB // KV_CB
    row = lambda b, c: b * nch + c
    prev = lambda b, c: jnp.maximum((b * nch + c) * (B_CHUNK // BLK) - 1, 0)
    cur_spec = lambda off: pl.BlockSpec((None, B_CHUNK, LANES),
                                        lambda b, kvc, c: (off // LANES + kvc, row(b, c), 0))
    prev_spec = lambda off: pl.BlockSpec((None, BLK, LANES),
                                         lambda b, kvc, c: (off // LANES + kvc, prev(b, c), 0))
    wide = lambda off: pl.BlockSpec((qpk, B_CHUNK, LANES),
                                    lambda b, kvc, c: (off // LANES // qpk + kvc, row(b, c), 0))
    return pl.pallas_call(
        _attn_b_kernel,
        grid=(batch, KV_CB, nch),
        in_specs=[pl.BlockSpec(memory_space=pltpu.SMEM),
                  wide(QB0), cur_spec(KB0), prev_spec(KB0), cur_spec(VB0), prev_spec(VB0), wide(GB0)],
        out_specs=[pl.BlockSpec((qpk, B_CHUNK, LANES), lambda b, kvc, c: (kvc, row(b, c), 0)), kv_spec, kv_spec],
        out_shape=[jax.ShapeDtypeStruct((B_CB, batch * seq, LANES), BF16), kv_shape, kv_shape],
        scratch_shapes=[pltpu.VMEM((2, BLK + B_CHUNK, LANES), BF16),
                        pltpu.VMEM((2, BLK + B_CHUNK, LANES), BF16),
                        pltpu.VMEM((2, Q_PER_KV, 2 * BLK, 2 * BLK), F32),
                        pltpu.VMEM((2, 2 * BLK, LANES), BF16)],
        compiler_params=_cparams(("parallel", "parallel", "arbitrary")),
        name="attn_b",
    )(sinks, z, z, z, z, z, z)


def _decode_kernel(zt_ref, ka_ref, va_ref, kb_ref, vb_ref, bias_a_ref, bias_b_ref, sink_ref, sel_ref, ones_ref,
                   ya_ref, yb_ref):
    n_h = A_HEADS
    zt = zt_ref[...]
    cols = lambda off, n: zt[:, off // HEAD_DIM:off // HEAD_DIM + n]
    eye = lax.broadcasted_iota(jnp.int32, (n_h, n_h), 0) == lax.broadcasted_iota(jnp.int32, (n_h, n_h), 1)
    to_col = lambda row: jnp.sum(jnp.where(eye, jnp.broadcast_to(row, (n_h, n_h)), 0.0), axis=1, keepdims=True)
    to_row = lambda c: jnp.sum(jnp.where(eye, jnp.broadcast_to(c, (n_h, n_h)), 0.0), axis=0, keepdims=True)

    def mixer(q, k_ref, v_ref, kv_head, bias, k_new, v_new, self_bias, sink):
        t = k_ref.shape[-1]
        parts = [jnp.sum((k_ref[kv_head(h)] * q[:, h:h + 1]).reshape(HEAD_DIM // SUBLANES, SUBLANES, t), axis=0)
                 for h in range(n_h)]
        s = jnp.dot(sel_ref[...], jnp.concatenate(parts, axis=0).astype(BF16),
                    preferred_element_type=F32) + bias
        s_self = jnp.sum(q * k_new, axis=0, keepdims=True) + self_bias
        m = jnp.maximum(to_row(jnp.max(s, axis=1, keepdims=True)), s_self)
        if sink is not None:
            m = jnp.maximum(m, sink)
        p = jnp.exp2(s - to_col(m))
        p_self = jnp.exp2(s_self - m)
        den = to_row(jnp.sum(p, axis=1, keepdims=True)) + p_self
        if sink is not None:
            den = den + jnp.exp2(sink - m)
        sums = []
        for h in range(n_h):
            w = v_ref[kv_head(h)] * p[h:h + 1, :]
            acc = w[:, :LANES]
            for j in range(1, t // LANES):
                acc = acc + w[:, j * LANES:(j + 1) * LANES]
            sums.append(acc)
        o = jnp.dot(jnp.concatenate(sums, axis=1).astype(BF16), ones_ref[...], preferred_element_type=F32)
        return (o + v_new * p_self) / den

    oa = mixer(cols(QA0, A_HEADS), ka_ref, va_ref, lambda h: h, bias_a_ref[...], cols(KA0, A_HEADS),
               cols(VA0, A_HEADS), math.log2(len(DILATIONS)), None)
    ya_ref[...] = (oa * _silu(cols(GA0, A_HEADS))).astype(BF16)

    kv_of_lane = lax.broadcasted_iota(jnp.int32, (HEAD_DIM, B_HEADS), 1) // Q_PER_KV
    per_query = lambda off: sum(jnp.where(kv_of_lane == g, zt[:, off // HEAD_DIM + g:off // HEAD_DIM + g + 1], 0.0)
                                for g in range(B_KV_HEADS))
    ob = mixer(cols(QB0, B_HEADS), kb_ref, vb_ref, lambda h: h // Q_PER_KV, bias_b_ref[...], per_query(KB0),
               per_query(VB0), 0.0, sink_ref[...])
    yb_ref[...] = (ob * _silu(cols(GB0, B_HEADS))).astype(BF16)


def _decode(z_s, cache_a_k, cache_a_v, cache_b_k, cache_b_v, sinks):
    n = z_s.shape[0]
    assert cache_a_k.shape[1] == A_REACH and cache_b_k.shape[1] == B_WINDOW == N_KEYS
    slopes = lambda nh: jnp.exp2(-8.0 * jnp.arange(1, nh + 1, dtype=F32) / nh)
    dd = A_REACH - jnp.arange(A_REACH)
    mult = sum(((dd % d == 0) & (dd <= w)).astype(F32) for w, d in DILATIONS)
    bias_a = jnp.where(mult > 0, -slopes(A_HEADS)[:, None] * dd.astype(F32) * LOG2E
                       + jnp.log2(jnp.maximum(mult, 1.0)), NEG)
    bias_b = -slopes(B_HEADS)[:, None] * (B_WINDOW - jnp.arange(B_WINDOW)).astype(F32) * LOG2E
    by_head = lambda c: c.transpose(0, 2, 3, 1)
    buf_spec = lambda heads, rows: pl.BlockSpec((None, heads, HEAD_DIM, rows), lambda i: (i, 0, 0, 0))
    const = lambda a: pl.BlockSpec(a.shape, lambda i: (0,) * a.ndim)
    sink2 = sinks.reshape(1, B_HEADS) * LOG2E
    zt = z_s.reshape(n, D_IN // HEAD_DIM, HEAD_DIM).transpose(0, 2, 1)
    sel = (jnp.arange(A_HEADS * SUBLANES)[None, :] // SUBLANES == jnp.arange(A_HEADS)[:, None]).astype(BF16)
    ones = (jnp.arange(A_HEADS * LANES)[:, None] // LANES == jnp.arange(A_HEADS)[None, :]).astype(BF16)
    out_spec = pl.BlockSpec((None, HEAD_DIM, A_HEADS), lambda i: (i, 0, 0))
    yat, ybt = pl.pallas_call(
        _decode_kernel,
        grid=(n,),
        in_specs=[pl.BlockSpec((None, HEAD_DIM, D_IN // HEAD_DIM), lambda i: (i, 0, 0)),
                  buf_spec(A_HEADS, A_REACH), buf_spec(A_HEADS, A_REACH),
                  buf_spec(B_KV_HEADS, B_WINDOW), buf_spec(B_KV_HEADS, B_WINDOW),
                  const(bias_a), const(bias_b), const(sink2), const(sel), const(ones)],
        out_specs=[out_spec, out_spec],
        out_shape=[jax.ShapeDtypeStruct((n, HEAD_DIM, A_HEADS), BF16)] * 2,
        compiler_params=_cparams(("parallel",)),
        name="decode",
    )(zt, by_head(cache_a_k), by_head(cache_a_v), by_head(cache_b_k), by_head(cache_b_v), bias_a, bias_b, sink2,
      sel, ones)
    return tuple(yt.transpose(0, 2, 1).reshape(n, A_WIDTH) for yt in (yat, ybt))


def _out_kernel(x_ref, ya_ref, yb_ref, p_ref, wout_ref, gple_ref, wpg_ref, wple_ref, gfin_ref, o_ref):
    y = jnp.concatenate([ya_ref[c] for c in range(A_CB)] + [yb_ref[c] for c in range(B_CB)], axis=1)
    h = x_ref[...] + jnp.dot(y, wout_ref[...], preferred_element_type=F32)
    gate = jax.nn.sigmoid(jnp.dot(_rms(h, gple_ref[...]).astype(BF16), wpg_ref[...],
                                  preferred_element_type=F32))
    ple = jnp.dot(p_ref[...].astype(BF16), wple_ref[...], preferred_element_type=F32)
    o_ref[...] = _rms(h + ple * gate, gfin_ref[...])


def _out(x, ya, yb, p, wout_bf, g_ple, wpg_bf, wple_bf, g_final, tm):
    rows, d = x.shape
    resident = lambda a: pl.BlockSpec(a.shape, lambda i: (0,) * a.ndim, pipeline_mode=pl.Buffered(1))
    g_ple, g_final = g_ple.reshape(1, d), g_final.reshape(1, d)
    return pl.pallas_call(
        _out_kernel,
        grid=(rows // tm,),
        in_specs=[pl.BlockSpec((tm, d), lambda i: (i, 0)),
                  pl.BlockSpec((A_CB, tm, LANES), lambda i: (0, i, 0)),
                  pl.BlockSpec((B_CB, tm, LANES), lambda i: (0, i, 0)),
                  pl.BlockSpec((tm, p.shape[1]), lambda i: (i, 0)),
                  resident(wout_bf), resident(g_ple), resident(wpg_bf), resident(wple_bf), resident(g_final)],
        out_specs=pl.BlockSpec((tm, d), lambda i: (i, 0)),
        out_shape=jax.ShapeDtypeStruct((rows, d), F32),
        compiler_params=_cparams(("parallel",)),
        name="out",
    )(x, ya, yb, p, wout_bf, g_ple, wpg_bf, wple_bf, g_final)


def _col_blocks(y):
    return y.reshape(y.shape[0], y.shape[1] // LANES, LANES).transpose(1, 0, 2)


def kernel(x_prompt, x_sample, cache_a_k, cache_a_v, cache_b_k, cache_b_v, p_prompt, p_sample,
           g_mix, w_in, sinks, w_out, g_ple, w_pg, w_ple, g_final):
    depth = w_in.shape[0]
    batch, seq, d = x_prompt.shape
    n_s, t_s, _ = x_sample.shape
    assert depth == 1 and t_s == 1 and w_in.shape[2] == D_IN
    xp = x_prompt.reshape(batch * seq, d)
    xs = x_sample.reshape(n_s, d)
    i = 0
    w_out_bf, w_pg_bf, w_ple_bf = (w[i].astype(BF16) for w in (w_out, w_pg, w_ple))

    z = _proj(xp, g_mix[i], w_in[i], tm=1024)
    ya, ak_t, av_t = _attn_a(z, batch, seq)
    yb, bk_t, bv_t = _attn_b(z, sinks[i], batch, seq)
    y_prompt = _out(xp, ya, yb, p_prompt[i].reshape(batch * seq, -1), w_out_bf, g_ple[i], w_pg_bf, w_ple_bf,
                    g_final, tm=256)
    new_p = [t.transpose(0, 3, 1, 2)[None] for t in (ak_t, av_t, bk_t, bv_t)]

    z_s = _proj(xs, g_mix[i], w_in[i], tm=n_s).transpose(1, 0, 2).reshape(n_s, D_IN)
    ya_s, yb_s = _decode(z_s, cache_a_k[i], cache_a_v[i], cache_b_k[i], cache_b_v[i], sinks[i])
    y_sample = _out(xs, _col_blocks(ya_s), _col_blocks(yb_s), p_sample[i].reshape(n_s, -1), w_out_bf, g_ple[i],
                    w_pg_bf, w_ple_bf, g_final, tm=n_s)
    new_s = [z_s[:, off:off + width].reshape(1, n_s, 1, width // HEAD_DIM, HEAD_DIM)
             for off, width in ((KA0, A_WIDTH), (VA0, A_WIDTH), (KB0, B_KV_WIDTH), (VB0, B_KV_WIDTH))]

    return (y_prompt.reshape(batch, seq, d), y_sample.reshape(n_s, t_s, d), *new_p, *new_s)
```

```python
import functools
import math

import jax
import jax.numpy as jnp
from jax import lax
from jax.experimental import pallas as pl
from jax.experimental.pallas import tpu as pltpu

F32 = jnp.float32
BF16 = jnp.bfloat16

LANES = 128
SUBLANES = 8
HEAD_DIM = 64
A_HEADS = 16
B_HEADS = 16
B_KV_HEADS = 4
BLK = 128
N_KEYS = 128
DILATIONS = ((128, 1), (512, 4), (2048, 16))
A_REACH = 2048
B_WINDOW = 128
EPS = 1e-6
NEG = -1e30

A_WIDTH = A_HEADS * HEAD_DIM
B_WIDTH = B_HEADS * HEAD_DIM
B_KV_WIDTH = B_KV_HEADS * HEAD_DIM
IN_SPLITS = (A_WIDTH, A_WIDTH, A_WIDTH, A_WIDTH, B_WIDTH, B_KV_WIDTH, B_KV_WIDTH, B_WIDTH)
D_IN = sum(IN_SPLITS)
N_CB = D_IN // LANES
QA0, KA0, VA0, GA0, QB0, KB0, VB0, GB0 = (sum(IN_SPLITS[:i]) for i in range(8))
A_CB = A_WIDTH // LANES
B_CB = B_WIDTH // LANES
KV_CB = B_KV_WIDTH // LANES
Q_PER_KV = B_HEADS // B_KV_HEADS

PROJ_TN = 512
B_CHUNK = 1024
A_UNITS = 16
B_UNITS = 16
LOG2E = math.log2(math.e)
Q_SCALE = HEAD_DIM ** -0.5 * LOG2E
VMEM_LIMIT = 56 * 1024 * 1024


def _cparams(sem):
    return pltpu.CompilerParams(dimension_semantics=sem, vmem_limit_bytes=VMEM_LIMIT)


def _rms(x, g):
    return x * lax.rsqrt(jnp.mean(x * x, axis=-1, keepdims=True) + EPS) * g


def _silu(g):
    return g * jax.nn.sigmoid(g)


def _proj_tile(hn, w_ref, z_ref):
    acc = jnp.dot(hn, w_ref[...].astype(BF16), preferred_element_type=F32)
    col = pl.program_id(1) * PROJ_TN
    is_q = ((col >= QA0) & (col < QA0 + A_WIDTH)) | ((col >= QB0) & (col < QB0 + B_WIDTH))
    acc = acc * jnp.where(is_q, Q_SCALE, 1.0)
    for c in range(PROJ_TN // LANES):
        z_ref[c] = acc[:, c * LANES:(c + 1) * LANES]


def _proj_kernel(x_ref, g_ref, w_ref, z_ref, hn_ref):
    @pl.when(pl.program_id(1) == 0)
    def _():
        hn_ref[...] = _rms(x_ref[...], g_ref[...]).astype(BF16)

    _proj_tile(hn_ref[...], w_ref, z_ref)


def _proj(x, g, w, tm):
    rows, d = x.shape
    return pl.pallas_call(
        _proj_kernel,
        grid=(rows // tm, D_IN // PROJ_TN),
        in_specs=[pl.BlockSpec((tm, d), lambda i, j: (i, 0)),
                  pl.BlockSpec((1, d), lambda i, j: (0, 0)),
                  pl.BlockSpec((d, PROJ_TN), lambda i, j: (0, j))],
        out_specs=pl.BlockSpec((PROJ_TN // LANES, tm, LANES), lambda i, j: (j, i, 0)),
        out_shape=jax.ShapeDtypeStruct((N_CB, rows, LANES), F32),
        scratch_shapes=[pltpu.VMEM((tm, d), BF16)],
        compiler_params=_cparams(("parallel", "arbitrary")),
        name="proj",
    )(x, g.reshape(1, d), w)


def _band_bias(slope, step):
    qi = lax.broadcasted_iota(jnp.int32, (BLK, 2 * BLK), 0)
    kj = lax.broadcasted_iota(jnp.int32, (BLK, 2 * BLK), 1)
    dist = qi + BLK - kj
    valid = (dist >= 0) & (dist <= N_KEYS)
    return jnp.where(valid, -slope * (dist * step).astype(F32) * LOG2E, NEG)


def _head_slope(h, n_heads):
    return jnp.exp2(jnp.full((1, 1), -8.0 / n_heads, F32) * (h + 1).astype(F32))


def _low_lanes(rows):
    return lax.broadcasted_iota(jnp.int32, (rows, LANES), 1) < HEAD_DIM


def _fill_ones(ones_s):
    ones_s[0] = jnp.where(_low_lanes(2 * BLK), 1.0, 0.0).astype(BF16)
    ones_s[1] = jnp.where(_low_lanes(2 * BLK), 0.0, 1.0).astype(BF16)


def _run_blocks(units, load, ones_s, finish, sinks_of=None):
    lo = _low_lanes(BLK)

    def scores(unit):
        q, kk, vv, bias = load(unit)
        zero = jnp.zeros_like(q)
        q2 = jnp.concatenate([jnp.where(lo, q, zero), jnp.where(lo, zero, q)], axis=0)
        return lax.dot_general(q2, kk, (((1,), (1,)), ((), ())), preferred_element_type=F32) + bias, vv

    def attend(s, vv, sinks):
        nk = vv.shape[0]
        halves = (s[:BLK], s[BLK:])
        m = [jnp.max(h, axis=1, keepdims=True) for h in halves]
        if sinks is not None:
            m = [jnp.maximum(mh, sk) for mh, sk in zip(m, sinks)]
        lhs = jnp.concatenate([jnp.exp2(h - mh).astype(BF16) for h, mh in zip(halves, m)], axis=1)
        zero = jnp.zeros_like(vv)
        rhs = jnp.concatenate(
            [jnp.concatenate([jnp.where(_low_lanes(nk), vv, zero), ones_s[0, :nk, :]], axis=1),
             jnp.concatenate([jnp.where(_low_lanes(nk), zero, vv), ones_s[1, :nk, :]], axis=1)], axis=0)
        ol = jnp.dot(lhs, rhs, preferred_element_type=F32)
        mt = jnp.where(lo, jnp.broadcast_to(m[0], (BLK, LANES)), jnp.broadcast_to(m[1], (BLK, LANES)))
        den = ol[:, LANES:]
        if sinks is not None:
            den = den + jnp.exp2(jnp.where(lo, sinks[0], sinks[1]) - mt)
        return ol[:, :LANES], den, mt

    results = []
    nxt = scores(units[0])
    for k, unit in enumerate(units):
        s, vv = nxt
        if k + 1 < len(units):
            nxt = scores(units[k + 1])
        results.append(attend(s, vv, None if sinks_of is None else sinks_of(unit)))
    for unit, res in zip(units, results):
        finish(unit, res)


def _attn_a_kernel(q_ref, k_ref, v_ref, g_ref, y_ref, kt_ref, vt_ref,
                   q1_s, k1_s, v1_s, q4_s, k4_s, v4_s, q16_s, k16_s, v16_s,
                   st16_s, st4_s, bias_s, ones_s, *, seq):
    keep = kt_ref.shape[-1]
    for src, dst in ((k_ref, kt_ref), (v_ref, vt_ref)):
        for j in range(keep // LANES):
            tile = src[seq - keep + j * LANES:seq - keep + (j + 1) * LANES, :]
            dst[:, :, j * LANES:(j + 1) * LANES] = tile.T.reshape(2, HEAD_DIM, LANES)

    hp = pl.program_id(1)
    steps = tuple(d for _, d in DILATIONS)

    for pi, step in enumerate(steps):
        for hh in range(2):
            bias_s[pi, hh * BLK:(hh + 1) * BLK, :] = _band_bias(_head_slope(2 * hp + hh, A_HEADS), step)
    _fill_ones(ones_s)

    for src, dsts in ((q_ref, (q1_s, q4_s, q16_s)), (k_ref, (k1_s, k4_s, k16_s)), (v_ref, (v1_s, v4_s, v16_s))):
        dsts[0][0] = src[...].astype(BF16)
        for dst, d in zip(dsts[1:], steps[1:]):
            for r in range(d):
                dst[r] = src[pl.ds(r, seq // d, stride=d), :].astype(BF16)

    def merge(a, b):
        m = jnp.maximum(a[2], b[2])
        wa = jnp.exp2(a[2] - m)
        wb = jnp.exp2(b[2] - m)
        return wa * a[0] + wb * b[0], wa * a[1] + wb * b[1], m

    def blocks(pi, refs, n_res, finish):
        nb = seq // steps[pi] // BLK
        q_s, k_s, v_s = refs

        def load(unit):
            r, row0, first = unit
            if first:
                keys, bias = pl.ds(row0, BLK), bias_s[pi, :, BLK:]
            else:
                keys, bias = pl.ds(row0 - BLK, 2 * BLK), bias_s[pi]
            return q_s[r, pl.ds(row0, BLK), :], k_s[r, keys, :], v_s[r, keys, :], bias

        def run(units):
            _run_blocks(units, load, ones_s, finish)

        def trips(n, unit_of):
            if n == 0:
                return
            per = max(u for u in range(1, A_UNITS + 1) if n % u == 0)
            if n == per:
                run([unit_of(i) for i in range(n)])
                return

            def trip(i, c):
                run([unit_of(i * per + u) for u in range(per)])
                return c
            lax.fori_loop(0, n // per, trip, 0)

        row = lambda b: b * BLK if isinstance(b, int) else pl.multiple_of(b * BLK, BLK)
        if n_res > 1:
            trips(n_res, lambda idx: (idx, 0, True))
            trips(n_res * (nb - 1), lambda idx: (idx // (nb - 1), row(idx % (nb - 1) + 1), False))
        else:
            run([(0, 0, True)] + [(0, row(b), False) for b in range(1, A_UNITS)])
            trips(nb - A_UNITS, lambda idx: (0, row(idx + A_UNITS), False))

    def finish16(unit, res):
        r, row0, _ = unit
        rows = pl.ds(r // 4 + 4 * row0, BLK, stride=4)
        for c in range(3):
            st16_s[c, r % 4, rows, :] = res[c]
    blocks(2, (q16_s, k16_s, v16_s), 16, finish16)

    def finish4(unit, res):
        r, row0, _ = unit
        res = merge(tuple(st16_s[c, r, pl.ds(row0, BLK), :] for c in range(3)), res)
        rows = pl.ds(r + 4 * row0, BLK, stride=4)
        for c in range(3):
            st4_s[c, rows, :] = res[c]
    blocks(1, (q4_s, k4_s, v4_s), 4, finish4)

    def finish1(unit, res):
        _, row0, _ = unit
        rows = pl.ds(row0, BLK)
        o, l, _ = merge(tuple(st4_s[c, rows, :] for c in range(3)), res)
        y_ref[rows, :] = (o / l * _silu(g_ref[rows, :])).astype(BF16)
    blocks(0, (q1_s, k1_s, v1_s), 1, finish1)


def _attn_a(z, batch, seq):
    assert DILATIONS == ((N_KEYS, 1), (4 * N_KEYS, 4), (16 * N_KEYS, 16)) and N_KEYS == BLK
    assert seq % (16 * BLK) == 0 and seq // 16 >= 2 * BLK and seq // BLK >= A_UNITS
    keep = min(A_REACH, seq)
    spec = lambda off: pl.BlockSpec((None, seq, LANES), lambda b, hp: (off // LANES + hp, b, 0))
    res = lambda d, dt: pltpu.VMEM((d, seq // d, LANES), dt)
    kv_spec = pl.BlockSpec((None, 2, HEAD_DIM, keep), lambda b, hp: (b, hp, 0, 0))
    kv_shape = jax.ShapeDtypeStruct((batch, A_HEADS, HEAD_DIM, keep), F32)
    return pl.pallas_call(
        functools.partial(_attn_a_kernel, seq=seq),
        grid=(batch, A_CB),
        in_specs=[spec(QA0), spec(KA0), spec(VA0), spec(GA0)],
        out_specs=[pl.BlockSpec((None, seq, LANES), lambda b, hp: (hp, b, 0)), kv_spec, kv_spec],
        out_shape=[jax.ShapeDtypeStruct((A_CB, batch * seq, LANES), BF16), kv_shape, kv_shape],
        scratch_shapes=[res(d, BF16) for d in (1, 1, 1, 4, 4, 4, 16, 16, 16)]
                       + [pltpu.VMEM((3, 4, seq // 4, LANES), F32), pltpu.VMEM((3, seq, LANES), F32),
                          pltpu.VMEM((3, 2 * BLK, 2 * BLK), F32), pltpu.VMEM((2, 2 * BLK, LANES), BF16)],
        compiler_params=_cparams(("parallel", "parallel")),
        name="attn_a",
    )(z, z, z, z)


def _attn_b_kernel(sink_ref, q_ref, kc_ref, kp_ref, vc_ref, vp_ref, g_ref, y_ref, kt_ref, vt_ref,
                   kk_s, vv_s, bias_s, ones_s):
    kvc = pl.program_id(1)
    ch = pl.program_id(2)
    heads = 2 * Q_PER_KV
    q_blocks = heads // 2

    @pl.when(ch == pl.num_programs(2) - 1)
    def _():
        keep = kt_ref.shape[-1]
        for src, dst in ((kc_ref, kt_ref), (vc_ref, vt_ref)):
            dst[...] = src[B_CHUNK - keep:, :].T.reshape(2, HEAD_DIM, keep)

    for cur, prev, dst in ((kc_ref, kp_ref, kk_s), (vc_ref, vp_ref, vv_s)):
        for src, a, n in ((prev, 0, BLK), (cur, BLK, B_CHUNK)):
            x = src[...]
            swapped = pltpu.roll(x, HEAD_DIM, 1)
            dst[0, a:a + n, :] = jnp.where(_low_lanes(n), x, swapped).astype(BF16)
            dst[1, a:a + n, :] = jnp.where(_low_lanes(n), swapped, x).astype(BF16)
    _fill_ones(ones_s)

    in_prev = lax.broadcasted_iota(jnp.int32, (BLK, 2 * BLK), 1) < BLK
    for hl in range(heads):
        full = _band_bias(_head_slope(kvc * heads + hl, B_HEADS), 1)
        rows = slice(hl % 2 * BLK, (hl % 2 + 1) * BLK)
        bias_s[0, hl // 2, rows, :] = full
        bias_s[1, hl // 2, rows, :] = jnp.where(in_prev, NEG, full)
    at_start = (ch == 0).astype(jnp.int32)

    def load(unit):
        i, qc = unit
        g = 2 * qc // Q_PER_KV
        keys = pl.ds(i * BLK, 2 * BLK)
        return (q_ref[qc, pl.ds(i * BLK, BLK), :].astype(BF16), kk_s[g, keys, :], vv_s[g, keys, :],
                bias_s[at_start if i == 0 else 0, qc])

    def sinks_of(unit):
        return tuple(sink_ref[kvc * heads + 2 * unit[1] + hh] * LOG2E for hh in range(2))

    def finish(unit, res):
        i, qc = unit
        rows = pl.ds(i * BLK, BLK)
        y_ref[qc, rows, :] = (res[0] / res[1] * _silu(g_ref[qc, rows, :])).astype(BF16)

    units = [(i, qc) for i in range(B_CHUNK // BLK) for qc in range(q_blocks)]
    for a in range(0, len(units), B_UNITS):
        _run_blocks(units[a:a + B_UNITS], load, ones_s, finish, sinks_of)


def _attn_b(z, sinks, batch, seq):
    assert seq % B_CHUNK == 0 and B_WINDOW == BLK
    nch = seq // B_CHUNK
    keep = min(B_WINDOW, seq)
    kv_spec = pl.BlockSpec((None, 2, HEAD_DIM, keep), lambda b, kvc, c: (b, kvc, 0, 0))
    kv_shape = jax.ShapeDtypeStruct((batch, B_KV_HEADS, HEAD_DIM, keep), F32)
    qpk = B_CB // KV_CB
    row = lambda b, c: b * nch + c
    prev = lambda b, c: jnp.maximum((b * nch + c) * (B_CHUNK // BLK) - 1, 0)
    cur_spec = lambda off: pl.BlockSpec((None, B_CHUNK, LANES),
                                        lambda b, kvc, c: (off // LANES + kvc, row(b, c), 0))
    prev_spec = lambda off: pl.BlockSpec((None, BLK, LANES),
                                         lambda b, kvc, c: (off // LANES + kvc, prev(b, c), 0))
    wide = lambda off: pl.BlockSpec((qpk, B_CHUNK, LANES),
                                    lambda b, kvc, c: (off // LANES // qpk + kvc, row(b, c), 0))
    return pl.pallas_call(
        _attn_b_kernel,
        grid=(batch, KV_CB, nch),
        in_specs=[pl.BlockSpec(memory_space=pltpu.SMEM),
                  wide(QB0), cur_spec(KB0), prev_spec(KB0), cur_spec(VB0), prev_spec(VB0), wide(GB0)],
        out_specs=[pl.BlockSpec((qpk, B_CHUNK, LANES), lambda b, kvc, c: (kvc, row(b, c), 0)), kv_spec, kv_spec],
        out_shape=[jax.ShapeDtypeStruct((B_CB, batch * seq, LANES), BF16), kv_shape, kv_shape],
        scratch_shapes=[pltpu.VMEM((2, BLK + B_CHUNK, LANES), BF16),
                        pltpu.VMEM((2, BLK + B_CHUNK, LANES), BF16),
                        pltpu.VMEM((2, Q_PER_KV, 2 * BLK, 2 * BLK), F32),
                        pltpu.VMEM((2, 2 * BLK, LANES), BF16)],
        compiler_params=_cparams(("parallel", "parallel", "arbitrary")),
        name="attn_b",
    )(sinks, z, z, z, z, z, z)


def _decode_kernel(zt_ref, ka_ref, va_ref, kb_ref, vb_ref, bias_a_ref, bias_b_ref, sink_ref, sel_ref, ones_ref,
                   ya_ref, yb_ref):
    n_h = A_HEADS
    zt = zt_ref[...]
    cols = lambda off, n: zt[:, off // HEAD_DIM:off // HEAD_DIM + n]
    eye = lax.broadcasted_iota(jnp.int32, (n_h, n_h), 0) == lax.broadcasted_iota(jnp.int32, (n_h, n_h), 1)
    to_col = lambda row: jnp.sum(jnp.where(eye, jnp.broadcast_to(row, (n_h, n_h)), 0.0), axis=1, keepdims=True)
    to_row = lambda c: jnp.sum(jnp.where(eye, jnp.broadcast_to(c, (n_h, n_h)), 0.0), axis=0, keepdims=True)

    def mixer(q, k_ref, v_ref, kv_head, bias, k_new, v_new, self_bias, sink):
        t = k_ref.shape[-1]
        parts = [jnp.sum((k_ref[kv_head(h)] * q[:, h:h + 1]).reshape(HEAD_DIM // SUBLANES, SUBLANES, t), axis=0)
                 for h in range(n_h)]
        s = jnp.dot(sel_ref[...], jnp.concatenate(parts, axis=0).astype(BF16),
                    preferred_element_type=F32) + bias
        s_self = jnp.sum(q * k_new, axis=0, keepdims=True) + self_bias
        m = jnp.maximum(to_row(jnp.max(s, axis=1, keepdims=True)), s_self)
        if sink is not None:
            m = jnp.maximum(m, sink)
        p = jnp.exp2(s - to_col(m))
        p_self = jnp.exp2(s_self - m)
        den = to_row(jnp.sum(p, axis=1, keepdims=True)) + p_self
        if sink is not None:
            den = den + jnp.exp2(sink - m)
        sums = []
        for h in range(n_h):
            w = v_ref[kv_head(h)] * p[h:h + 1, :]
            acc = w[:, :LANES]
            for j in range(1, t // LANES):
                acc = acc + w[:, j * LANES:(j + 1) * LANES]
            sums.append(acc)
        o = jnp.dot(jnp.concatenate(sums, axis=1).astype(BF16), ones_ref[...], preferred_element_type=F32)
        return (o + v_new * p_self) / den

    oa = mixer(cols(QA0, A_HEADS), ka_ref, va_ref, lambda h: h, bias_a_ref[...], cols(KA0, A_HEADS),
               cols(VA0, A_HEADS), math.log2(len(DILATIONS)), None)
    ya_ref[...] = (oa * _silu(cols(GA0, A_HEADS))).astype(BF16)

    kv_of_lane = lax.broadcasted_iota(jnp.int32, (HEAD_DIM, B_HEADS), 1) // Q_PER_KV
    per_query = lambda off: sum(jnp.where(kv_of_lane == g, zt[:, off // HEAD_DIM + g:off // HEAD_DIM + g + 1], 0.0)
                                for g in range(B_KV_HEADS))
    ob = mixer(cols(QB0, B_HEADS), kb_ref, vb_ref, lambda h: h // Q_PER_KV, bias_b_ref[...], per_query(KB0),
               per_query(VB0), 0.0, sink_ref[...])
    yb_ref[...] = (ob * _silu(cols(GB0, B_HEADS))).astype(BF16)


def _decode_operands(z_s, cache_a_k, cache_a_v, cache_b_k, cache_b_v, sinks):
    n = z_s.shape[0]
    assert cache_a_k.shape[1] == A_REACH and cache_b_k.shape[1] == B_WINDOW == N_KEYS
    slopes = lambda nh: jnp.exp2(-8.0 * jnp.arange(1, nh + 1, dtype=F32) / nh)
    dd = A_REACH - jnp.arange(A_REACH)
    mult = sum(((dd % d == 0) & (dd <= w)).astype(F32) for w, d in DILATIONS)
    bias_a = jnp.where(mult > 0, -slopes(A_HEADS)[:, None] * dd.astype(F32) * LOG2E
                       + jnp.log2(jnp.maximum(mult, 1.0)), NEG)
    bias_b = -slopes(B_HEADS)[:, None] * (B_WINDOW - jnp.arange(B_WINDOW)).astype(F32) * LOG2E
    by_head = lambda c: c.transpose(0, 2, 3, 1)
    sink2 = sinks.reshape(1, B_HEADS) * LOG2E
    zt = z_s.reshape(n, D_IN // HEAD_DIM, HEAD_DIM).transpose(0, 2, 1)
    sel = (jnp.arange(A_HEADS * SUBLANES)[None, :] // SUBLANES == jnp.arange(A_HEADS)[:, None]).astype(BF16)
    ones = (jnp.arange(A_HEADS * LANES)[:, None] // LANES == jnp.arange(A_HEADS)[None, :]).astype(BF16)
    per_seq = (zt, by_head(cache_a_k), by_head(cache_a_v), by_head(cache_b_k), by_head(cache_b_v))
    return per_seq, (bias_a, bias_b, sink2, sel, ones)


def _decode_specs(operands, seq_of):
    per_seq, consts = operands
    seq_spec = lambda a: pl.BlockSpec((None,) + a.shape[1:], lambda *g: (seq_of(*g),) + (0,) * (a.ndim - 1))
    const = lambda a: pl.BlockSpec(a.shape, lambda *g: (0,) * a.ndim)
    out_spec = pl.BlockSpec((None, HEAD_DIM, A_HEADS), lambda *g: (seq_of(*g), 0, 0))
    out_shape = lambda n: [jax.ShapeDtypeStruct((n, HEAD_DIM, A_HEADS), BF16)] * 2
    return [seq_spec(a) for a in per_seq] + [const(a) for a in consts], [out_spec, out_spec], out_shape


def _decode(operands, first, count):
    in_specs, out_specs, out_shape = _decode_specs(operands, lambda s: s + first)
    return pl.pallas_call(
        _decode_kernel,
        grid=(count,),
        in_specs=in_specs,
        out_specs=[pl.BlockSpec((None, HEAD_DIM, A_HEADS), lambda s: (s, 0, 0))] * 2,
        out_shape=out_shape(count),
        compiler_params=_cparams(("parallel",)),
        name="decode",
    )(*operands[0], *operands[1])


def _norm_kernel(x_ref, g_ref, o_ref):
    o_ref[...] = _rms(x_ref[...], g_ref[...]).astype(BF16)


def _norm(x, g, tm):
    rows, d = x.shape
    return pl.pallas_call(
        _norm_kernel,
        grid=(rows // tm,),
        in_specs=[pl.BlockSpec((tm, d), lambda i: (i, 0)), pl.BlockSpec((1, d), lambda i: (0, 0))],
        out_specs=pl.BlockSpec((tm, d), lambda i: (i, 0)),
        out_shape=jax.ShapeDtypeStruct((rows, d), BF16),
        compiler_params=_cparams(("parallel",)),
        name="norm",
    )(x, g.reshape(1, d))


def _proj_decode_kernel(hn_ref, w_ref, *refs):
    *dec_in, z_ref, ya_ref, yb_ref = refs
    _proj_tile(hn_ref[...], w_ref, z_ref)
    _decode_kernel(*dec_in, ya_ref, yb_ref)


def _proj_decode(hn, w_bf, operands, tm):
    rows, d = hn.shape
    n_col = D_IN // PROJ_TN
    steps = rows // tm * n_col
    assert steps <= operands[0][0].shape[0]
    in_specs, out_specs, out_shape = _decode_specs(operands, lambda i, j: i * n_col + j)
    return pl.pallas_call(
        _proj_decode_kernel,
        grid=(rows // tm, n_col),
        in_specs=[pl.BlockSpec((tm, d), lambda i, j: (i, 0), pipeline_mode=pl.Buffered(1)),
                  pl.BlockSpec((d, PROJ_TN), lambda i, j: (0, j))] + in_specs,
        out_specs=[pl.BlockSpec((PROJ_TN // LANES, tm, LANES), lambda i, j: (j, i, 0))] + out_specs,
        out_shape=[jax.ShapeDtypeStruct((N_CB, rows, LANES), F32)] + out_shape(steps),
        compiler_params=_cparams(("parallel", "arbitrary")),
        name="proj_decode",
    )(hn, w_bf, *operands[0], *operands[1])


def _out_kernel(x_ref, ya_ref, yb_ref, p_ref, wout_ref, gple_ref, wpg_ref, wple_ref, gfin_ref, o_ref):
    y = jnp.concatenate([ya_ref[c] for c in range(A_CB)] + [yb_ref[c] for c in range(B_CB)], axis=1)
    h = x_ref[...] + jnp.dot(y, wout_ref[...], preferred_element_type=F32)
    gate = jax.nn.sigmoid(jnp.dot(_rms(h, gple_ref[...]).astype(BF16), wpg_ref[...],
                                  preferred_element_type=F32))
    ple = jnp.dot(p_ref[...].astype(BF16), wple_ref[...], preferred_element_type=F32)
    o_ref[...] = _rms(h + ple * gate, gfin_ref[...])


def _out(x, ya, yb, p, wout_bf, g_ple, wpg_bf, wple_bf, g_final, tm):
    rows, d = x.shape
    resident = lambda a: pl.BlockSpec(a.shape, lambda i: (0,) * a.ndim, pipeline_mode=pl.Buffered(1))
    g_ple, g_final = g_ple.reshape(1, d), g_final.reshape(1, d)
    return pl.pallas_call(
        _out_kernel,
        grid=(rows // tm,),
        in_specs=[pl.BlockSpec((tm, d), lambda i: (i, 0)),
                  pl.BlockSpec((A_CB, tm, LANES), lambda i: (0, i, 0)),
                  pl.BlockSpec((B_CB, tm, LANES), lambda i: (0, i, 0)),
                  pl.BlockSpec((tm, p.shape[1]), lambda i: (i, 0)),
                  resident(wout_bf), resident(g_ple), resident(wpg_bf), resident(wple_bf), resident(g_final)],
        out_specs=pl.BlockSpec((tm, d), lambda i: (i, 0)),
        out_shape=jax.ShapeDtypeStruct((rows, d), F32),
        compiler_params=_cparams(("parallel",)),
        name="out",
    )(x, ya, yb, p, wout_bf, g_ple, wpg_bf, wple_bf, g_final)


def _col_blocks(y):
    return y.reshape(y.shape[0], y.shape[1] // LANES, LANES).transpose(1, 0, 2)


def kernel(x_prompt, x_sample, cache_a_k, cache_a_v, cache_b_k, cache_b_v, p_prompt, p_sample,
           g_mix, w_in, sinks, w_out, g_ple, w_pg, w_ple, g_final):
    depth = w_in.shape[0]
    batch, seq, d = x_prompt.shape
    n_s, t_s, _ = x_sample.shape
    assert depth == 1 and t_s == 1 and w_in.shape[2] == D_IN
    xp = x_prompt.reshape(batch * seq, d)
    xs = x_sample.reshape(n_s, d)
    i = 0
    w_in_bf, w_out_bf, w_pg_bf, w_ple_bf = (w[i].astype(BF16) for w in (w_in, w_out, w_pg, w_ple))

    z_s = _proj(xs, g_mix[i], w_in_bf, tm=n_s).transpose(1, 0, 2).reshape(n_s, D_IN)
    dec = _decode_operands(z_s, cache_a_k[i], cache_a_v[i], cache_b_k[i], cache_b_v[i], sinks[i])

    z, *y_s0 = _proj_decode(_norm(xp, g_mix[i], tm=512), w_in_bf, dec, tm=1024)
    n_rode = y_s0[0].shape[0]
    y_s1 = _decode(dec, n_rode, n_s - n_rode)
    ya_s, yb_s = (jnp.concatenate(parts).transpose(0, 2, 1).reshape(n_s, A_WIDTH) for parts in zip(y_s0, y_s1))
    ya, ak_t, av_t = _attn_a(z, batch, seq)
    yb, bk_t, bv_t = _attn_b(z, sinks[i], batch, seq)
    y_prompt = _out(xp, ya, yb, p_prompt[i].reshape(batch * seq, -1), w_out_bf, g_ple[i], w_pg_bf, w_ple_bf,
                    g_final, tm=256)
    new_p = [t.transpose(0, 3, 1, 2)[None] for t in (ak_t, av_t, bk_t, bv_t)]

    y_sample = _out(xs, _col_blocks(ya_s), _col_blocks(yb_s), p_sample[i].reshape(n_s, -1), w_out_bf, g_ple[i],
                    w_pg_bf, w_ple_bf, g_final, tm=n_s)
    new_s = [z_s[:, off:off + width].reshape(1, n_s, 1, width // HEAD_DIM, HEAD_DIM)
             for off, width in ((KA0, A_WIDTH), (VA0, A_WIDTH), (KB0, B_KV_WIDTH), (VB0, B_KV_WIDTH))]

    return (y_prompt.reshape(batch, seq, d), y_sample.reshape(n_s, t_s, d), *new_p, *new_s)
```

```python
import functools
import math

import jax
import jax.numpy as jnp
from jax import lax
from jax.experimental import pallas as pl
from jax.experimental.pallas import tpu as pltpu

F32 = jnp.float32
BF16 = jnp.bfloat16

LANES = 128
SUBLANES = 8
HEAD_DIM = 64
A_HEADS = 16
B_HEADS = 16
B_KV_HEADS = 4
BLK = 128
N_KEYS = 128
DILATIONS = ((128, 1), (512, 4), (2048, 16))
A_REACH = 2048
B_WINDOW = 128
EPS = 1e-6
NEG = -1e30

A_WIDTH = A_HEADS * HEAD_DIM
B_WIDTH = B_HEADS * HEAD_DIM
B_KV_WIDTH = B_KV_HEADS * HEAD_DIM
IN_SPLITS = (A_WIDTH, A_WIDTH, A_WIDTH, A_WIDTH, B_WIDTH, B_KV_WIDTH, B_KV_WIDTH, B_WIDTH)
D_IN = sum(IN_SPLITS)
N_CB = D_IN // LANES
QA0, KA0, VA0, GA0, QB0, KB0, VB0, GB0 = (sum(IN_SPLITS[:i]) for i in range(8))
A_CB = A_WIDTH // LANES
B_CB = B_WIDTH // LANES
KV_CB = B_KV_WIDTH // LANES
Q_PER_KV = B_HEADS // B_KV_HEADS

PROJ_TN = 512
B_CHUNK = 1024
A_UNITS = 16
B_UNITS = 16
LOG2E = math.log2(math.e)
Q_SCALE = HEAD_DIM ** -0.5 * LOG2E
VMEM_LIMIT = 56 * 1024 * 1024


def _cparams(sem):
    return pltpu.CompilerParams(dimension_semantics=sem, vmem_limit_bytes=VMEM_LIMIT)


def _rms(x, g):
    return x * lax.rsqrt(jnp.mean(x * x, axis=-1, keepdims=True) + EPS) * g


def _silu(g):
    return g * jax.nn.sigmoid(g)


def _proj_tile(hn, w_ref, z_ref):
    acc = jnp.dot(hn, w_ref[...].astype(BF16), preferred_element_type=F32)
    col = pl.program_id(1) * PROJ_TN
    is_q = ((col >= QA0) & (col < QA0 + A_WIDTH)) | ((col >= QB0) & (col < QB0 + B_WIDTH))
    acc = acc * jnp.where(is_q, Q_SCALE, 1.0)
    for c in range(PROJ_TN // LANES):
        z_ref[c] = acc[:, c * LANES:(c + 1) * LANES]


def _proj_kernel(x_ref, g_ref, w_ref, z_ref, hn_ref):
    @pl.when(pl.program_id(1) == 0)
    def _():
        hn_ref[...] = _rms(x_ref[...], g_ref[...]).astype(BF16)

    _proj_tile(hn_ref[...], w_ref, z_ref)


def _proj(x, g, w, tm):
    rows, d = x.shape
    return pl.pallas_call(
        _proj_kernel,
        grid=(rows // tm, D_IN // PROJ_TN),
        in_specs=[pl.BlockSpec((tm, d), lambda i, j: (i, 0)),
                  pl.BlockSpec((1, d), lambda i, j: (0, 0)),
                  pl.BlockSpec((d, PROJ_TN), lambda i, j: (0, j))],
        out_specs=pl.BlockSpec((PROJ_TN // LANES, tm, LANES), lambda i, j: (j, i, 0)),
        out_shape=jax.ShapeDtypeStruct((N_CB, rows, LANES), F32),
        scratch_shapes=[pltpu.VMEM((tm, d), BF16)],
        compiler_params=_cparams(("parallel", "arbitrary")),
        name="proj",
    )(x, g.reshape(1, d), w)


def _band_bias(slope, step):
    qi = lax.broadcasted_iota(jnp.int32, (BLK, 2 * BLK), 0)
    kj = lax.broadcasted_iota(jnp.int32, (BLK, 2 * BLK), 1)
    dist = qi + BLK - kj
    valid = (dist >= 0) & (dist <= N_KEYS)
    return jnp.where(valid, -slope * (dist * step).astype(F32) * LOG2E, NEG)


def _head_slope(h, n_heads):
    return jnp.exp2(jnp.full((1, 1), -8.0 / n_heads, F32) * (h + 1).astype(F32))


def _low_lanes(rows):
    return lax.broadcasted_iota(jnp.int32, (rows, LANES), 1) < HEAD_DIM


def _fill_ones(ones_s):
    ones_s[0] = jnp.where(_low_lanes(2 * BLK), 1.0, 0.0).astype(BF16)
    ones_s[1] = jnp.where(_low_lanes(2 * BLK), 0.0, 1.0).astype(BF16)


def _run_blocks(units, load, ones_s, finish, sinks_of=None):
    lo = _low_lanes(BLK)

    def scores(unit):
        q, kk, vv, bias = load(unit)
        zero = jnp.zeros_like(q)
        q2 = jnp.concatenate([jnp.where(lo, q, zero), jnp.where(lo, zero, q)], axis=0)
        return lax.dot_general(q2, kk, (((1,), (1,)), ((), ())), preferred_element_type=F32) + bias, vv

    def attend(s, vv, sinks):
        nk = vv.shape[0]
        halves = (s[:BLK], s[BLK:])
        m = [jnp.max(h, axis=1, keepdims=True) for h in halves]
        if sinks is not None:
            m = [jnp.maximum(mh, sk) for mh, sk in zip(m, sinks)]
        lhs = jnp.concatenate([jnp.exp2(h - mh).astype(BF16) for h, mh in zip(halves, m)], axis=1)
        zero = jnp.zeros_like(vv)
        rhs = jnp.concatenate(
            [jnp.concatenate([jnp.where(_low_lanes(nk), vv, zero), ones_s[0, :nk, :]], axis=1),
             jnp.concatenate([jnp.where(_low_lanes(nk), zero, vv), ones_s[1, :nk, :]], axis=1)], axis=0)
        ol = jnp.dot(lhs, rhs, preferred_element_type=F32)
        mt = jnp.where(lo, jnp.broadcast_to(m[0], (BLK, LANES)), jnp.broadcast_to(m[1], (BLK, LANES)))
        den = ol[:, LANES:]
        if sinks is not None:
            den = den + jnp.exp2(jnp.where(lo, sinks[0], sinks[1]) - mt)
        return ol[:, :LANES], den, mt

    results = []
    nxt = scores(units[0])
    for k, unit in enumerate(units):
        s, vv = nxt
        if k + 1 < len(units):
            nxt = scores(units[k + 1])
        results.append(attend(s, vv, None if sinks_of is None else sinks_of(unit)))
    for unit, res in zip(units, results):
        finish(unit, res)


def _attn_a_kernel(q_ref, k_ref, v_ref, g_ref, y_ref, kt_ref, vt_ref,
                   q1_s, k1_s, v1_s, q4_s, k4_s, v4_s, q16_s, k16_s, v16_s,
                   by4_s, st16_s, st4_s, bias_s, ones_s, *, seq):
    keep = kt_ref.shape[-1]
    for src, dst in ((k_ref, kt_ref), (v_ref, vt_ref)):
        for j in range(keep // LANES):
            tile = src[seq - keep + j * LANES:seq - keep + (j + 1) * LANES, :]
            dst[:, :, j * LANES:(j + 1) * LANES] = tile.T.reshape(2, HEAD_DIM, LANES)

    hp = pl.program_id(1)
    steps = tuple(d for _, d in DILATIONS)

    for pi, step in enumerate(steps):
        for hh in range(2):
            bias_s[pi, hh * BLK:(hh + 1) * BLK, :] = _band_bias(_head_slope(2 * hp + hh, A_HEADS), step)
    _fill_ones(ones_s)

    assert steps == (1, 4, 16)
    for src, dsts in ((q_ref, (q1_s, q4_s, q16_s)), (k_ref, (k1_s, k4_s, k16_s)), (v_ref, (v1_s, v4_s, v16_s))):
        dsts[0][0] = src[...].astype(BF16)
        for r in range(4):
            rows = src[pl.ds(r, seq // 4, stride=4), :]
            by4_s[r] = rows
            dsts[1][r] = rows.astype(BF16)
        for r in range(16):
            dsts[2][r] = by4_s[r % 4, pl.ds(r // 4, seq // 16, stride=4), :].astype(BF16)

    def merge(a, b):
        m = jnp.maximum(a[2], b[2])
        wa = jnp.exp2(a[2] - m)
        wb = jnp.exp2(b[2] - m)
        return wa * a[0] + wb * b[0], wa * a[1] + wb * b[1], m

    def blocks(pi, refs, n_res, finish):
        nb = seq // steps[pi] // BLK
        q_s, k_s, v_s = refs

        def load(unit):
            r, row0, first = unit
            if first:
                keys, bias = pl.ds(row0, BLK), bias_s[pi, :, BLK:]
            else:
                keys, bias = pl.ds(row0 - BLK, 2 * BLK), bias_s[pi]
            return q_s[r, pl.ds(row0, BLK), :], k_s[r, keys, :], v_s[r, keys, :], bias

        def run(units):
            _run_blocks(units, load, ones_s, finish)

        def trips(n, unit_of):
            if n == 0:
                return
            per = max(u for u in range(1, A_UNITS + 1) if n % u == 0)
            if n == per:
                run([unit_of(i) for i in range(n)])
                return

            def trip(i, c):
                run([unit_of(i * per + u) for u in range(per)])
                return c
            lax.fori_loop(0, n // per, trip, 0)

        row = lambda b: b * BLK if isinstance(b, int) else pl.multiple_of(b * BLK, BLK)
        if n_res > 1:
            trips(n_res, lambda idx: (idx, 0, True))
            trips(n_res * (nb - 1), lambda idx: (idx // (nb - 1), row(idx % (nb - 1) + 1), False))
        else:
            run([(0, 0, True)] + [(0, row(b), False) for b in range(1, A_UNITS)])
            trips(nb - A_UNITS, lambda idx: (0, row(idx + A_UNITS), False))

    def finish16(unit, res):
        r, row0, _ = unit
        rows = pl.ds(r // 4 + 4 * row0, BLK, stride=4)
        for c in range(3):
            st16_s[c, r % 4, rows, :] = res[c]
    blocks(2, (q16_s, k16_s, v16_s), 16, finish16)

    def finish4(unit, res):
        r, row0, _ = unit
        res = merge(tuple(st16_s[c, r, pl.ds(row0, BLK), :] for c in range(3)), res)
        rows = pl.ds(r + 4 * row0, BLK, stride=4)
        for c in range(3):
            st4_s[c, rows, :] = res[c]
    blocks(1, (q4_s, k4_s, v4_s), 4, finish4)

    def finish1(unit, res):
        _, row0, _ = unit
        rows = pl.ds(row0, BLK)
        o, l, _ = merge(tuple(st4_s[c, rows, :] for c in range(3)), res)
        y_ref[rows, :] = (o / l * _silu(g_ref[rows, :])).astype(BF16)
    blocks(0, (q1_s, k1_s, v1_s), 1, finish1)


def _attn_a(z, batch, seq):
    assert DILATIONS == ((N_KEYS, 1), (4 * N_KEYS, 4), (16 * N_KEYS, 16)) and N_KEYS == BLK
    assert seq % (16 * BLK) == 0 and seq // 16 >= 2 * BLK and seq // BLK >= A_UNITS
    keep = min(A_REACH, seq)
    spec = lambda off: pl.BlockSpec((None, seq, LANES), lambda b, hp: (off // LANES + hp, b, 0))
    res = lambda d, dt: pltpu.VMEM((d, seq // d, LANES), dt)
    kv_spec = pl.BlockSpec((None, 2, HEAD_DIM, keep), lambda b, hp: (b, hp, 0, 0))
    kv_shape = jax.ShapeDtypeStruct((batch, A_HEADS, HEAD_DIM, keep), F32)
    return pl.pallas_call(
        functools.partial(_attn_a_kernel, seq=seq),
        grid=(batch, A_CB),
        in_specs=[spec(QA0), spec(KA0), spec(VA0), spec(GA0)],
        out_specs=[pl.BlockSpec((None, seq, LANES), lambda b, hp: (hp, b, 0)), kv_spec, kv_spec],
        out_shape=[jax.ShapeDtypeStruct((A_CB, batch * seq, LANES), BF16), kv_shape, kv_shape],
        scratch_shapes=[res(d, BF16) for d in (1, 1, 1, 4, 4, 4, 16, 16, 16)]
                       + [res(4, F32), pltpu.VMEM((3, 4, seq // 4, LANES), F32), pltpu.VMEM((3, seq, LANES), F32),
                          pltpu.VMEM((3, 2 * BLK, 2 * BLK), F32), pltpu.VMEM((2, 2 * BLK, LANES), BF16)],
        compiler_params=_cparams(("parallel", "parallel")),
        name="attn_a",
    )(z, z, z, z)


def _attn_b_kernel(sink_ref, q_ref, kc_ref, kp_ref, vc_ref, vp_ref, g_ref, y_ref, kt_ref, vt_ref,
                   kk_s, vv_s, bias_s, ones_s):
    kvc = pl.program_id(1)
    ch = pl.program_id(2)
    heads = 2 * Q_PER_KV
    q_blocks = heads // 2

    @pl.when(ch == pl.num_programs(2) - 1)
    def _():
        keep = kt_ref.shape[-1]
        for src, dst in ((kc_ref, kt_ref), (vc_ref, vt_ref)):
            dst[...] = src[B_CHUNK - keep:, :].T.reshape(2, HEAD_DIM, keep)

    for cur, prev, dst in ((kc_ref, kp_ref, kk_s), (vc_ref, vp_ref, vv_s)):
        for src, a, n in ((prev, 0, BLK), (cur, BLK, B_CHUNK)):
            x = src[...]
            swapped = pltpu.roll(x, HEAD_DIM, 1)
            dst[0, a:a + n, :] = jnp.where(_low_lanes(n), x, swapped).astype(BF16)
            dst[1, a:a + n, :] = jnp.where(_low_lanes(n), swapped, x).astype(BF16)
    _fill_ones(ones_s)

    in_prev = lax.broadcasted_iota(jnp.int32, (BLK, 2 * BLK), 1) < BLK
    for hl in range(heads):
        full = _band_bias(_head_slope(kvc * heads + hl, B_HEADS), 1)
        rows = slice(hl % 2 * BLK, (hl % 2 + 1) * BLK)
        bias_s[0, hl // 2, rows, :] = full
        bias_s[1, hl // 2, rows, :] = jnp.where(in_prev, NEG, full)
    at_start = (ch == 0).astype(jnp.int32)

    def load(unit):
        i, qc = unit
        g = 2 * qc // Q_PER_KV
        keys = pl.ds(i * BLK, 2 * BLK)
        return (q_ref[qc, pl.ds(i * BLK, BLK), :].astype(BF16), kk_s[g, keys, :], vv_s[g, keys, :],
                bias_s[at_start if i == 0 else 0, qc])

    def sinks_of(unit):
        return tuple(sink_ref[kvc * heads + 2 * unit[1] + hh] * LOG2E for hh in range(2))

    def finish(unit, res):
        i, qc = unit
        rows = pl.ds(i * BLK, BLK)
        y_ref[qc, rows, :] = (res[0] / res[1] * _silu(g_ref[qc, rows, :])).astype(BF16)

    units = [(i, qc) for i in range(B_CHUNK // BLK) for qc in range(q_blocks)]
    for a in range(0, len(units), B_UNITS):
        _run_blocks(units[a:a + B_UNITS], load, ones_s, finish, sinks_of)


def _attn_b(z, sinks, batch, seq):
    assert seq % B_CHUNK == 0 and B_WINDOW == BLK
    nch = seq // B_CHUNK
    keep = min(B_WINDOW, seq)
    kv_spec = pl.BlockSpec((None, 2, HEAD_DIM, keep), lambda b, kvc, c: (b, kvc, 0, 0))
    kv_shape = jax.ShapeDtypeStruct((batch, B_KV_HEADS, HEAD_DIM, keep), F32)
    qpk = B_CB // KV_CB
    row = lambda b, c: b * nch + c
    prev = lambda b, c: jnp.maximum((b * nch + c) * (B_CHUNK // BLK) - 1, 0)
    cur_spec = lambda off: pl.BlockSpec((None, B_CHUNK, LANES),
                                        lambda b, kvc, c: (off // LANES + kvc, row(b, c), 0))
    prev_spec = lambda off: pl.BlockSpec((None, BLK, LANES),
                                         lambda b, kvc, c: (off // LANES + kvc, prev(b, c), 0))
    wide = lambda off: pl.BlockSpec((qpk, B_CHUNK, LANES),
                                    lambda b, kvc, c: (off // LANES // qpk + kvc, row(b, c), 0))
    return pl.pallas_call(
        _attn_b_kernel,
        grid=(batch, KV_CB, nch),
        in_specs=[pl.BlockSpec(memory_space=pltpu.SMEM),
                  wide(QB0), cur_spec(KB0), prev_spec(KB0), cur_spec(VB0), prev_spec(VB0), wide(GB0)],
        out_specs=[pl.BlockSpec((qpk, B_CHUNK, LANES), lambda b, kvc, c: (kvc, row(b, c), 0)), kv_spec, kv_spec],
        out_shape=[jax.ShapeDtypeStruct((B_CB, batch * seq, LANES), BF16), kv_shape, kv_shape],
        scratch_shapes=[pltpu.VMEM((2, BLK + B_CHUNK, LANES), BF16),
                        pltpu.VMEM((2, BLK + B_CHUNK, LANES), BF16),
                        pltpu.VMEM((2, Q_PER_KV, 2 * BLK, 2 * BLK), F32),
                        pltpu.VMEM((2, 2 * BLK, LANES), BF16)],
        compiler_params=_cparams(("parallel", "parallel", "arbitrary")),
        name="attn_b",
    )(sinks, z, z, z, z, z, z)


def _decode_kernel(zt_ref, ka_ref, va_ref, kb_ref, vb_ref, bias_a_ref, bias_b_ref, sink_ref, sel_ref, ones_ref,
                   ya_ref, yb_ref):
    n_h, n_kv = ya_ref.shape[-1], kb_ref.shape[0]
    zt = zt_ref[...]
    split_at = dict(zip(("qa", "ka", "va", "ga", "qb", "kb", "vb", "gb"),
                        (0, n_h, 2 * n_h, 3 * n_h, 4 * n_h, 5 * n_h, 5 * n_h + n_kv, 5 * n_h + 2 * n_kv)))
    cols = lambda name: zt[:, split_at[name]:split_at[name] + n_h]
    eye = lax.broadcasted_iota(jnp.int32, (n_h, n_h), 0) == lax.broadcasted_iota(jnp.int32, (n_h, n_h), 1)
    to_col = lambda row: jnp.sum(jnp.where(eye, jnp.broadcast_to(row, (n_h, n_h)), 0.0), axis=1, keepdims=True)
    to_row = lambda c: jnp.sum(jnp.where(eye, jnp.broadcast_to(c, (n_h, n_h)), 0.0), axis=0, keepdims=True)

    def mixer(q, k_ref, v_ref, kv_head, bias, k_new, v_new, self_bias, sink):
        t = k_ref.shape[-1]
        parts = [jnp.sum((k_ref[kv_head(h)] * q[:, h:h + 1]).reshape(HEAD_DIM // SUBLANES, SUBLANES, t), axis=0)
                 for h in range(n_h)]
        s = jnp.dot(sel_ref[...], jnp.concatenate(parts, axis=0).astype(BF16),
                    preferred_element_type=F32) + bias
        s_self = jnp.sum(q * k_new, axis=0, keepdims=True) + self_bias
        m = jnp.maximum(to_row(jnp.max(s, axis=1, keepdims=True)), s_self)
        if sink is not None:
            m = jnp.maximum(m, sink)
        p = jnp.exp2(s - to_col(m))
        p_self = jnp.exp2(s_self - m)
        den = to_row(jnp.sum(p, axis=1, keepdims=True)) + p_self
        if sink is not None:
            den = den + jnp.exp2(sink - m)
        sums = []
        for h in range(n_h):
            w = v_ref[kv_head(h)] * p[h:h + 1, :]
            acc = w[:, :LANES]
            for j in range(1, t // LANES):
                acc = acc + w[:, j * LANES:(j + 1) * LANES]
            sums.append(acc)
        o = jnp.dot(jnp.concatenate(sums, axis=1).astype(BF16), ones_ref[...], preferred_element_type=F32)
        return (o + v_new * p_self) / den

    oa = mixer(cols("qa"), ka_ref, va_ref, lambda h: h, bias_a_ref[...], cols("ka"), cols("va"),
               math.log2(len(DILATIONS)), None)
    ya_ref[...] = (oa * _silu(cols("ga"))).astype(BF16)

    kv_of_lane = lax.broadcasted_iota(jnp.int32, (HEAD_DIM, n_h), 1) // Q_PER_KV
    per_query = lambda name: sum(jnp.where(kv_of_lane == g, zt[:, split_at[name] + g:split_at[name] + g + 1], 0.0)
                                 for g in range(n_kv))
    ob = mixer(cols("qb"), kb_ref, vb_ref, lambda h: h // Q_PER_KV, bias_b_ref[...], per_query("kb"),
               per_query("vb"), 0.0, sink_ref[...])
    yb_ref[...] = (ob * _silu(cols("gb"))).astype(BF16)


def _decode_operands(z_s, cache_a_k, cache_a_v, cache_b_k, cache_b_v, sinks, groups):
    n = z_s.shape[0]
    assert cache_a_k.shape[1] == A_REACH and cache_b_k.shape[1] == B_WINDOW == N_KEYS
    n_h, n_kv = A_HEADS // groups, B_KV_HEADS // groups
    assert A_HEADS == B_HEADS and n_h * groups == A_HEADS and n_kv * groups == B_KV_HEADS
    slopes = lambda nh: jnp.exp2(-8.0 * jnp.arange(1, nh + 1, dtype=F32) / nh)
    dd = A_REACH - jnp.arange(A_REACH)
    mult = sum(((dd % d == 0) & (dd <= w)).astype(F32) for w, d in DILATIONS)
    bias_a = jnp.where(mult > 0, -slopes(A_HEADS)[:, None] * dd.astype(F32) * LOG2E
                       + jnp.log2(jnp.maximum(mult, 1.0)), NEG)
    bias_b = -slopes(B_HEADS)[:, None] * (B_WINDOW - jnp.arange(B_WINDOW)).astype(F32) * LOG2E
    by_head = lambda c: c.transpose(0, 2, 3, 1)
    sink2 = (sinks * LOG2E).reshape(groups, 1, n_h)
    heads = z_s.reshape(n, D_IN // HEAD_DIM, HEAD_DIM)
    share = lambda g: jnp.concatenate(
        [heads[:, off // HEAD_DIM + g * w:off // HEAD_DIM + (g + 1) * w]
         for off, w in zip((QA0, KA0, VA0, GA0, QB0, KB0, VB0, GB0), (n_h,) * 5 + (n_kv,) * 2 + (n_h,))], axis=1)
    zt = jnp.stack([share(g) for g in range(groups)], axis=1).transpose(0, 1, 3, 2)
    sel = (jnp.arange(n_h * SUBLANES)[None, :] // SUBLANES == jnp.arange(n_h)[:, None]).astype(BF16)
    ones = (jnp.arange(n_h * LANES)[:, None] // LANES == jnp.arange(n_h)[None, :]).astype(BF16)
    return (zt, by_head(cache_a_k), by_head(cache_a_v), by_head(cache_b_k), by_head(cache_b_v),
            bias_a, bias_b, sink2, sel, ones)


def _decode_specs(operands, where):
    zt, ka, va, kb, vb, bias_a, bias_b, sink2, sel, ones = operands
    groups, n_h, n_kv = sink2.shape[0], sink2.shape[2], kb.shape[1] // sink2.shape[0]
    at = lambda pick: (lambda *g: pick(*where(*g)))
    buf_spec = lambda a, heads: pl.BlockSpec((None, heads) + a.shape[2:], at(lambda s, grp: (s, grp, 0, 0)))
    rows_spec = lambda a: pl.BlockSpec((n_h, a.shape[1]), at(lambda s, grp: (grp, 0)))
    const = lambda a: pl.BlockSpec(a.shape, lambda *g: (0,) * a.ndim)
    in_specs = [pl.BlockSpec((None, None) + zt.shape[2:], at(lambda s, grp: (s, grp, 0, 0))),
                buf_spec(ka, n_h), buf_spec(va, n_h), buf_spec(kb, n_kv), buf_spec(vb, n_kv),
                rows_spec(bias_a), rows_spec(bias_b),
                pl.BlockSpec((None, 1, n_h), at(lambda s, grp: (grp, 0, 0))), const(sel), const(ones)]
    out_shape = lambda n: [jax.ShapeDtypeStruct((n, groups, HEAD_DIM, n_h), BF16)] * 2
    return in_specs, out_shape


def _decode(operands, first, count):
    groups = operands[7].shape[0]
    n_h = operands[7].shape[2]
    in_specs, out_shape = _decode_specs(operands, lambda i: (first + i // groups, i % groups))
    out_spec = pl.BlockSpec((None, None, HEAD_DIM, n_h), lambda i: (i // groups, i % groups, 0, 0))
    return pl.pallas_call(
        _decode_kernel,
        grid=(count * groups,),
        in_specs=in_specs,
        out_specs=[out_spec, out_spec],
        out_shape=out_shape(count),
        compiler_params=_cparams(("parallel",)),
        name="decode",
    )(*operands)


def _norm_kernel(x_ref, g_ref, o_ref):
    o_ref[...] = _rms(x_ref[...], g_ref[...]).astype(BF16)


def _norm(x, g, tm):
    rows, d = x.shape
    return pl.pallas_call(
        _norm_kernel,
        grid=(rows // tm,),
        in_specs=[pl.BlockSpec((tm, d), lambda i: (i, 0)), pl.BlockSpec((1, d), lambda i: (0, 0))],
        out_specs=pl.BlockSpec((tm, d), lambda i: (i, 0)),
        out_shape=jax.ShapeDtypeStruct((rows, d), BF16),
        compiler_params=_cparams(("parallel",)),
        name="norm",
    )(x, g.reshape(1, d))


def _proj_decode_kernel(hn_ref, w_ref, *refs):
    *dec_in, z_ref, ya_ref, yb_ref = refs
    _proj_tile(hn_ref[...], w_ref, z_ref)
    _decode_kernel(*dec_in, ya_ref, yb_ref)


def _proj_decode(hn, w_bf, operands, tm):
    rows, d = hn.shape
    n_col = D_IN // PROJ_TN
    steps = rows // tm * n_col
    groups, n_h = operands[7].shape[0], operands[7].shape[2]
    assert groups == 1 and steps <= operands[0].shape[0]
    in_specs, out_shape = _decode_specs(operands, lambda i, j: (i * n_col + j, 0))
    y_spec = pl.BlockSpec((None, None, HEAD_DIM, n_h), lambda i, j: (i * n_col + j, 0, 0, 0))
    return pl.pallas_call(
        _proj_decode_kernel,
        grid=(rows // tm, n_col),
        in_specs=[pl.BlockSpec((tm, d), lambda i, j: (i, 0), pipeline_mode=pl.Buffered(1)),
                  pl.BlockSpec((d, PROJ_TN), lambda i, j: (0, j))] + in_specs,
        out_specs=[pl.BlockSpec((PROJ_TN // LANES, tm, LANES), lambda i, j: (j, i, 0)), y_spec, y_spec],
        out_shape=[jax.ShapeDtypeStruct((N_CB, rows, LANES), F32)] + out_shape(steps),
        compiler_params=_cparams(("parallel", "arbitrary")),
        name="proj_decode",
    )(hn, w_bf, *operands)


def _out_kernel(x_ref, ya_ref, yb_ref, p_ref, wout_ref, gple_ref, wpg_ref, wple_ref, gfin_ref, o_ref):
    y = jnp.concatenate([ya_ref[c] for c in range(A_CB)] + [yb_ref[c] for c in range(B_CB)], axis=1)
    h = x_ref[...] + jnp.dot(y, wout_ref[...], preferred_element_type=F32)
    gate = jax.nn.sigmoid(jnp.dot(_rms(h, gple_ref[...]).astype(BF16), wpg_ref[...],
                                  preferred_element_type=F32))
    ple = jnp.dot(p_ref[...].astype(BF16), wple_ref[...], preferred_element_type=F32)
    o_ref[...] = _rms(h + ple * gate, gfin_ref[...])


N_OUT_REFS = 10


def _out_decode_kernel(*refs):
    _out_kernel(*refs[:N_OUT_REFS - 1], refs[-3])
    _decode_kernel(*refs[N_OUT_REFS - 1:-3], refs[-2], refs[-1])


def _out(x, ya, yb, p, wout_bf, g_ple, wpg_bf, wple_bf, g_final, tm, riders=None, first=0):
    rows, d = x.shape
    resident = lambda a: pl.BlockSpec(a.shape, lambda i: (0,) * a.ndim, pipeline_mode=pl.Buffered(1))
    g_ple, g_final = g_ple.reshape(1, d), g_final.reshape(1, d)
    in_specs = [pl.BlockSpec((tm, d), lambda i: (i, 0)),
                pl.BlockSpec((A_CB, tm, LANES), lambda i: (0, i, 0)),
                pl.BlockSpec((B_CB, tm, LANES), lambda i: (0, i, 0)),
                pl.BlockSpec((tm, p.shape[1]), lambda i: (i, 0)),
                resident(wout_bf), resident(g_ple), resident(wpg_bf), resident(wple_bf), resident(g_final)]
    out_specs = [pl.BlockSpec((tm, d), lambda i: (i, 0))]
    out_shape = [jax.ShapeDtypeStruct((rows, d), F32)]
    operands = (x, ya, yb, p, wout_bf, g_ple, wpg_bf, wple_bf, g_final)
    assert len(operands) + 1 == N_OUT_REFS
    if riders is None:
        return pl.pallas_call(_out_kernel, grid=(rows // tm,), in_specs=in_specs, out_specs=out_specs[0],
                              out_shape=out_shape[0], compiler_params=_cparams(("parallel",)), name="out")(*operands)
    groups, n_h = riders[7].shape[0], riders[7].shape[2]
    steps = rows // tm
    assert steps % groups == 0 and first + steps // groups <= riders[0].shape[0]
    rider_specs, rider_shape = _decode_specs(riders, lambda i: (first + i // groups, i % groups))
    y_spec = pl.BlockSpec((None, None, HEAD_DIM, n_h), lambda i: (i // groups, i % groups, 0, 0))
    return pl.pallas_call(
        _out_decode_kernel,
        grid=(steps,),
        in_specs=in_specs + rider_specs,
        out_specs=out_specs + [y_spec, y_spec],
        out_shape=out_shape + rider_shape(steps // groups),
        compiler_params=_cparams(("arbitrary",)),
        name="out_decode",
    )(*operands, *riders)


def _col_blocks(y):
    return y.reshape(y.shape[0], y.shape[1] // LANES, LANES).transpose(1, 0, 2)


def kernel(x_prompt, x_sample, cache_a_k, cache_a_v, cache_b_k, cache_b_v, p_prompt, p_sample,
           g_mix, w_in, sinks, w_out, g_ple, w_pg, w_ple, g_final):
    depth = w_in.shape[0]
    batch, seq, d = x_prompt.shape
    n_s, t_s, _ = x_sample.shape
    assert depth == 1 and t_s == 1 and w_in.shape[2] == D_IN
    xp = x_prompt.reshape(batch * seq, d)
    xs = x_sample.reshape(n_s, d)
    i = 0
    w_in_bf, w_out_bf, w_pg_bf, w_ple_bf = (w[i].astype(BF16) for w in (w_in, w_out, w_pg, w_ple))

    z_s = _proj(xs, g_mix[i], w_in_bf, tm=n_s).transpose(1, 0, 2).reshape(n_s, D_IN)
    dec_whole, dec_halves = (_decode_operands(z_s, cache_a_k[i], cache_a_v[i], cache_b_k[i], cache_b_v[i],
                                              sinks[i], groups) for groups in (1, 2))

    z, *y_s0 = _proj_decode(_norm(xp, g_mix[i], tm=512), w_in_bf, dec_whole, tm=1024)
    ya, ak_t, av_t = _attn_a(z, batch, seq)
    yb, bk_t, bv_t = _attn_b(z, sinks[i], batch, seq)
    n0 = y_s0[0].shape[0]
    y_prompt, *y_s1 = _out(xp, ya, yb, p_prompt[i].reshape(batch * seq, -1), w_out_bf, g_ple[i], w_pg_bf, w_ple_bf,
                           g_final, tm=256, riders=dec_halves, first=n0)
    new_p = [t.transpose(0, 3, 1, 2)[None] for t in (ak_t, av_t, bk_t, bv_t)]

    n1 = n0 + y_s1[0].shape[0]
    y_s2 = _decode(dec_whole, n1, n_s - n1)
    rows_of = lambda t: t.transpose(0, 1, 3, 2).reshape(t.shape[0], A_WIDTH)
    ya_s, yb_s = (jnp.concatenate([rows_of(t) for t in parts]) for parts in zip(y_s0, y_s1, y_s2))
    y_sample = _out(xs, _col_blocks(ya_s), _col_blocks(yb_s), p_sample[i].reshape(n_s, -1), w_out_bf, g_ple[i],
                    w_pg_bf, w_ple_bf, g_final, tm=n_s)
    new_s = [z_s[:, off:off + width].reshape(1, n_s, 1, width // HEAD_DIM, HEAD_DIM)
             for off, width in ((KA0, A_WIDTH), (VA0, A_WIDTH), (KB0, B_KV_WIDTH), (VB0, B_KV_WIDTH))]

    return (y_prompt.reshape(batch, seq, d), y_sample.reshape(n_s, t_s, d), *new_p, *new_s)
```

```python
import functools
import math

import jax
import jax.numpy as jnp
from jax import lax
from jax.experimental import pallas as pl
from jax.experimental.pallas import tpu as pltpu

F32 = jnp.float32
BF16 = jnp.bfloat16

LANES = 128
SUBLANES = 8
HEAD_DIM = 64
A_HEADS = 16
B_HEADS = 16
B_KV_HEADS = 4
BLK = 128
N_KEYS = 128
DILATIONS = ((128, 1), (512, 4), (2048, 16))
A_REACH = 2048
B_WINDOW = 128
EPS = 1e-6
NEG = -1e30

A_WIDTH = A_HEADS * HEAD_DIM
B_WIDTH = B_HEADS * HEAD_DIM
B_KV_WIDTH = B_KV_HEADS * HEAD_DIM
IN_SPLITS = (A_WIDTH, A_WIDTH, A_WIDTH, A_WIDTH, B_WIDTH, B_KV_WIDTH, B_KV_WIDTH, B_WIDTH)
D_IN = sum(IN_SPLITS)
N_CB = D_IN // LANES
QA0, KA0, VA0, GA0, QB0, KB0, VB0, GB0 = (sum(IN_SPLITS[:i]) for i in range(8))
A_CB = A_WIDTH // LANES
B_CB = B_WIDTH // LANES
KV_CB = B_KV_WIDTH // LANES
Q_PER_KV = B_HEADS // B_KV_HEADS

PROJ_TN = 512
B_CHUNK = 1024
A_UNITS = 16
B_UNITS = 16
LOG2E = math.log2(math.e)
Q_SCALE = HEAD_DIM ** -0.5 * LOG2E
VMEM_LIMIT = 56 * 1024 * 1024


def _cparams(sem):
    return pltpu.CompilerParams(dimension_semantics=sem, vmem_limit_bytes=VMEM_LIMIT)


def _rms(x, g):
    return x * lax.rsqrt(jnp.mean(x * x, axis=-1, keepdims=True) + EPS) * g


def _silu(g):
    return g * jax.nn.sigmoid(g)


def _proj_tile(hn, w_ref, z_ref):
    acc = jnp.dot(hn, w_ref[...].astype(BF16), preferred_element_type=F32)
    col = pl.program_id(1) * PROJ_TN
    is_q = ((col >= QA0) & (col < QA0 + A_WIDTH)) | ((col >= QB0) & (col < QB0 + B_WIDTH))
    acc = acc * jnp.where(is_q, Q_SCALE, 1.0)
    for c in range(PROJ_TN // LANES):
        z_ref[c] = acc[:, c * LANES:(c + 1) * LANES]


def _proj_kernel(x_ref, g_ref, w_ref, z_ref, hn_ref):
    @pl.when(pl.program_id(1) == 0)
    def _():
        hn_ref[...] = _rms(x_ref[...], g_ref[...]).astype(BF16)

    _proj_tile(hn_ref[...], w_ref, z_ref)


def _proj(x, g, w, tm):
    rows, d = x.shape
    return pl.pallas_call(
        _proj_kernel,
        grid=(rows // tm, D_IN // PROJ_TN),
        in_specs=[pl.BlockSpec((tm, d), lambda i, j: (i, 0)),
                  pl.BlockSpec((1, d), lambda i, j: (0, 0)),
                  pl.BlockSpec((d, PROJ_TN), lambda i, j: (0, j))],
        out_specs=pl.BlockSpec((PROJ_TN // LANES, tm, LANES), lambda i, j: (j, i, 0)),
        out_shape=jax.ShapeDtypeStruct((N_CB, rows, LANES), F32),
        scratch_shapes=[pltpu.VMEM((tm, d), BF16)],
        compiler_params=_cparams(("parallel", "arbitrary")),
        name="proj",
    )(x, g.reshape(1, d), w)


def _band_bias(slope, step):
    qi = lax.broadcasted_iota(jnp.int32, (BLK, 2 * BLK), 0)
    kj = lax.broadcasted_iota(jnp.int32, (BLK, 2 * BLK), 1)
    dist = qi + BLK - kj
    valid = (dist >= 0) & (dist <= N_KEYS)
    return jnp.where(valid, -slope * (dist * step).astype(F32) * LOG2E, NEG)


def _head_slope(h, n_heads):
    return jnp.exp2(jnp.full((1, 1), -8.0 / n_heads, F32) * (h + 1).astype(F32))


def _low_lanes(rows):
    return lax.broadcasted_iota(jnp.int32, (rows, LANES), 1) < HEAD_DIM


def _fill_ones(ones_s):
    ones_s[0] = jnp.where(_low_lanes(2 * BLK), 1.0, 0.0).astype(BF16)
    ones_s[1] = jnp.where(_low_lanes(2 * BLK), 0.0, 1.0).astype(BF16)


def _run_blocks(units, load, ones_s, finish, sinks_of=None):
    lo = _low_lanes(BLK)

    def scores(unit):
        q, kk, vv, bias = load(unit)
        zero = jnp.zeros_like(q)
        q2 = jnp.concatenate([jnp.where(lo, q, zero), jnp.where(lo, zero, q)], axis=0)
        return lax.dot_general(q2, kk, (((1,), (1,)), ((), ())), preferred_element_type=F32) + bias, vv

    def attend(s, vv, sinks):
        nk = vv.shape[0]
        halves = (s[:BLK], s[BLK:])
        m = [jnp.max(h, axis=1, keepdims=True) for h in halves]
        if sinks is not None:
            m = [jnp.maximum(mh, sk) for mh, sk in zip(m, sinks)]
        lhs = jnp.concatenate([jnp.exp2(h - mh).astype(BF16) for h, mh in zip(halves, m)], axis=1)
        zero = jnp.zeros_like(vv)
        rhs = jnp.concatenate(
            [jnp.concatenate([jnp.where(_low_lanes(nk), vv, zero), ones_s[0, :nk, :]], axis=1),
             jnp.concatenate([jnp.where(_low_lanes(nk), zero, vv), ones_s[1, :nk, :]], axis=1)], axis=0)
        ol = jnp.dot(lhs, rhs, preferred_element_type=F32)
        mt = jnp.where(lo, jnp.broadcast_to(m[0], (BLK, LANES)), jnp.broadcast_to(m[1], (BLK, LANES)))
        den = ol[:, LANES:]
        if sinks is not None:
            den = den + jnp.exp2(jnp.where(lo, sinks[0], sinks[1]) - mt)
        return ol[:, :LANES], den, mt

    results = []
    nxt = scores(units[0])
    for k, unit in enumerate(units):
        s, vv = nxt
        if k + 1 < len(units):
            nxt = scores(units[k + 1])
        results.append(attend(s, vv, None if sinks_of is None else sinks_of(unit)))
    for unit, res in zip(units, results):
        finish(unit, res)


def _attn_a_kernel(q_ref, k_ref, v_ref, g_ref, y_ref, kt_ref, vt_ref,
                   q1_s, k1_s, v1_s, q4_s, k4_s, v4_s, q16_s, k16_s, v16_s,
                   by4_s, st16_s, st4_s, bias_s, ones_s, *, seq):
    keep = kt_ref.shape[-1]
    for src, dst in ((k_ref, kt_ref), (v_ref, vt_ref)):
        for j in range(keep // LANES):
            tile = src[seq - keep + j * LANES:seq - keep + (j + 1) * LANES, :]
            dst[:, :, j * LANES:(j + 1) * LANES] = tile.T.reshape(2, HEAD_DIM, LANES)

    hp = pl.program_id(1)
    steps = tuple(d for _, d in DILATIONS)

    for pi, step in enumerate(steps):
        for hh in range(2):
            bias_s[pi, hh * BLK:(hh + 1) * BLK, :] = _band_bias(_head_slope(2 * hp + hh, A_HEADS), step)
    _fill_ones(ones_s)

    assert steps == (1, 4, 16)
    for src, dsts in ((q_ref, (q1_s, q4_s, q16_s)), (k_ref, (k1_s, k4_s, k16_s)), (v_ref, (v1_s, v4_s, v16_s))):
        dsts[0][0] = src[...].astype(BF16)
        for r in range(4):
            rows = src[pl.ds(r, seq // 4, stride=4), :]
            by4_s[r] = rows
            dsts[1][r] = rows.astype(BF16)
        for r in range(16):
            dsts[2][r] = by4_s[r % 4, pl.ds(r // 4, seq // 16, stride=4), :].astype(BF16)

    def merge(a, b):
        m = jnp.maximum(a[2], b[2])
        wa = jnp.exp2(a[2] - m)
        wb = jnp.exp2(b[2] - m)
        return wa * a[0] + wb * b[0], wa * a[1] + wb * b[1], m

    def blocks(pi, refs, n_res, finish):
        nb = seq // steps[pi] // BLK
        q_s, k_s, v_s = refs

        def load(unit):
            r, row0, first = unit
            if first:
                keys, bias = pl.ds(row0, BLK), bias_s[pi, :, BLK:]
            else:
                keys, bias = pl.ds(row0 - BLK, 2 * BLK), bias_s[pi]
            return q_s[r, pl.ds(row0, BLK), :], k_s[r, keys, :], v_s[r, keys, :], bias

        def run(units):
            _run_blocks(units, load, ones_s, finish)

        def trips(n, unit_of):
            if n == 0:
                return
            per = max(u for u in range(1, A_UNITS + 1) if n % u == 0)
            if n == per:
                run([unit_of(i) for i in range(n)])
                return

            def trip(i, c):
                run([unit_of(i * per + u) for u in range(per)])
                return c
            lax.fori_loop(0, n // per, trip, 0)

        row = lambda b: b * BLK if isinstance(b, int) else pl.multiple_of(b * BLK, BLK)
        if n_res > 1:
            trips(n_res, lambda idx: (idx, 0, True))
            trips(n_res * (nb - 1), lambda idx: (idx // (nb - 1), row(idx % (nb - 1) + 1), False))
        else:
            run([(0, 0, True)] + [(0, row(b), False) for b in range(1, A_UNITS)])
            trips(nb - A_UNITS, lambda idx: (0, row(idx + A_UNITS), False))

    def finish16(unit, res):
        r, row0, _ = unit
        rows = pl.ds(r // 4 + 4 * row0, BLK, stride=4)
        for c in range(3):
            st16_s[c, r % 4, rows, :] = res[c]
    blocks(2, (q16_s, k16_s, v16_s), 16, finish16)

    def finish4(unit, res):
        r, row0, _ = unit
        res = merge(tuple(st16_s[c, r, pl.ds(row0, BLK), :] for c in range(3)), res)
        rows = pl.ds(r + 4 * row0, BLK, stride=4)
        for c in range(3):
            st4_s[c, rows, :] = res[c]
    blocks(1, (q4_s, k4_s, v4_s), 4, finish4)

    def finish1(unit, res):
        _, row0, _ = unit
        rows = pl.ds(row0, BLK)
        o, l, _ = merge(tuple(st4_s[c, rows, :] for c in range(3)), res)
        y_ref[rows, :] = (o / l * _silu(g_ref[rows, :])).astype(BF16)
    blocks(0, (q1_s, k1_s, v1_s), 1, finish1)


def _attn_a(z, batch, seq):
    assert DILATIONS == ((N_KEYS, 1), (4 * N_KEYS, 4), (16 * N_KEYS, 16)) and N_KEYS == BLK
    assert seq % (16 * BLK) == 0 and seq // 16 >= 2 * BLK and seq // BLK >= A_UNITS
    keep = min(A_REACH, seq)
    spec = lambda off: pl.BlockSpec((None, seq, LANES), lambda b, hp: (off // LANES + hp, b, 0))
    res = lambda d, dt: pltpu.VMEM((d, seq // d, LANES), dt)
    kv_spec = pl.BlockSpec((None, 2, HEAD_DIM, keep), lambda b, hp: (b, hp, 0, 0))
    kv_shape = jax.ShapeDtypeStruct((batch, A_HEADS, HEAD_DIM, keep), F32)
    return pl.pallas_call(
        functools.partial(_attn_a_kernel, seq=seq),
        grid=(batch, A_CB),
        in_specs=[spec(QA0), spec(KA0), spec(VA0), spec(GA0)],
        out_specs=[pl.BlockSpec((None, seq, LANES), lambda b, hp: (hp, b, 0)), kv_spec, kv_spec],
        out_shape=[jax.ShapeDtypeStruct((A_CB, batch * seq, LANES), BF16), kv_shape, kv_shape],
        scratch_shapes=[res(d, BF16) for d in (1, 1, 1, 4, 4, 4, 16, 16, 16)]
                       + [res(4, F32), pltpu.VMEM((3, 4, seq // 4, LANES), F32), pltpu.VMEM((3, seq, LANES), F32),
                          pltpu.VMEM((3, 2 * BLK, 2 * BLK), F32), pltpu.VMEM((2, 2 * BLK, LANES), BF16)],
        compiler_params=_cparams(("parallel", "parallel")),
        name="attn_a",
    )(z, z, z, z)


def _attn_b_kernel(sink_ref, q_ref, kc_ref, kp_ref, vc_ref, vp_ref, g_ref, y_ref, kt_ref, vt_ref,
                   kk_s, vv_s, bias_s, ones_s):
    kvc = pl.program_id(1)
    ch = pl.program_id(2)
    heads = 2 * Q_PER_KV
    q_blocks = heads // 2

    @pl.when(ch == pl.num_programs(2) - 1)
    def _():
        keep = kt_ref.shape[-1]
        for src, dst in ((kc_ref, kt_ref), (vc_ref, vt_ref)):
            dst[...] = src[B_CHUNK - keep:, :].T.reshape(2, HEAD_DIM, keep)

    for cur, prev, dst in ((kc_ref, kp_ref, kk_s), (vc_ref, vp_ref, vv_s)):
        for src, a, n in ((prev, 0, BLK), (cur, BLK, B_CHUNK)):
            x = src[...]
            swapped = pltpu.roll(x, HEAD_DIM, 1)
            dst[0, a:a + n, :] = jnp.where(_low_lanes(n), x, swapped).astype(BF16)
            dst[1, a:a + n, :] = jnp.where(_low_lanes(n), swapped, x).astype(BF16)
    _fill_ones(ones_s)

    in_prev = lax.broadcasted_iota(jnp.int32, (BLK, 2 * BLK), 1) < BLK
    for hl in range(heads):
        full = _band_bias(_head_slope(kvc * heads + hl, B_HEADS), 1)
        rows = slice(hl % 2 * BLK, (hl % 2 + 1) * BLK)
        bias_s[0, hl // 2, rows, :] = full
        bias_s[1, hl // 2, rows, :] = jnp.where(in_prev, NEG, full)
    at_start = (ch == 0).astype(jnp.int32)

    def load(unit):
        i, qc = unit
        g = 2 * qc // Q_PER_KV
        keys = pl.ds(i * BLK, 2 * BLK)
        return (q_ref[qc, pl.ds(i * BLK, BLK), :].astype(BF16), kk_s[g, keys, :], vv_s[g, keys, :],
                bias_s[at_start if i == 0 else 0, qc])

    def sinks_of(unit):
        return tuple(sink_ref[kvc * heads + 2 * unit[1] + hh] * LOG2E for hh in range(2))

    def finish(unit, res):
        i, qc = unit
        rows = pl.ds(i * BLK, BLK)
        y_ref[qc, rows, :] = (res[0] / res[1] * _silu(g_ref[qc, rows, :])).astype(BF16)

    units = [(i, qc) for i in range(B_CHUNK // BLK) for qc in range(q_blocks)]
    for a in range(0, len(units), B_UNITS):
        _run_blocks(units[a:a + B_UNITS], load, ones_s, finish, sinks_of)


N_B_IN, N_B_OUT = 7, 3


def _attn_b_decode_kernel(*refs):
    n_in = len(refs) - N_B_OUT - 2 - 4
    b_out = refs[n_in:n_in + N_B_OUT]
    _attn_b_kernel(*refs[:N_B_IN], *b_out, *refs[-4:])
    _decode_kernel(*refs[N_B_IN:n_in], *refs[n_in + N_B_OUT:n_in + N_B_OUT + 2])


def _attn_b(z, sinks, batch, seq, riders=None, first=0):
    assert seq % B_CHUNK == 0 and B_WINDOW == BLK
    nch = seq // B_CHUNK
    keep = min(B_WINDOW, seq)
    kv_spec = pl.BlockSpec((None, 2, HEAD_DIM, keep), lambda b, kvc, c: (b, kvc, 0, 0))
    kv_shape = jax.ShapeDtypeStruct((batch, B_KV_HEADS, HEAD_DIM, keep), F32)
    qpk = B_CB // KV_CB
    row = lambda b, c: b * nch + c
    prev = lambda b, c: jnp.maximum((b * nch + c) * (B_CHUNK // BLK) - 1, 0)
    cur_spec = lambda off: pl.BlockSpec((None, B_CHUNK, LANES),
                                        lambda b, kvc, c: (off // LANES + kvc, row(b, c), 0))
    prev_spec = lambda off: pl.BlockSpec((None, BLK, LANES),
                                         lambda b, kvc, c: (off // LANES + kvc, prev(b, c), 0))
    wide = lambda off: pl.BlockSpec((qpk, B_CHUNK, LANES),
                                    lambda b, kvc, c: (off // LANES // qpk + kvc, row(b, c), 0))
    in_specs = [pl.BlockSpec(memory_space=pltpu.SMEM),
                wide(QB0), cur_spec(KB0), prev_spec(KB0), cur_spec(VB0), prev_spec(VB0), wide(GB0)]
    out_specs = [pl.BlockSpec((qpk, B_CHUNK, LANES), lambda b, kvc, c: (kvc, row(b, c), 0)), kv_spec, kv_spec]
    out_shape = [jax.ShapeDtypeStruct((B_CB, batch * seq, LANES), BF16), kv_shape, kv_shape]
    operands = (sinks, z, z, z, z, z, z)
    assert len(in_specs) == N_B_IN and len(out_specs) == N_B_OUT
    body = _attn_b_kernel
    if riders is not None:
        groups, n_h = riders[7].shape[0], riders[7].shape[2]
        steps = batch * KV_CB * nch
        step = lambda b, kvc, c: (b * KV_CB + kvc) * nch + c
        assert steps % groups == 0 and first + steps // groups <= riders[0].shape[0]
        rider_specs, rider_shape = _decode_specs(
            riders, lambda b, kvc, c: (first + step(b, kvc, c) // groups, step(b, kvc, c) % groups))
        y_spec = pl.BlockSpec((None, None, HEAD_DIM, n_h),
                              lambda b, kvc, c: (step(b, kvc, c) // groups, step(b, kvc, c) % groups, 0, 0))
        in_specs, out_specs = in_specs + rider_specs, out_specs + [y_spec, y_spec]
        out_shape, operands = out_shape + rider_shape(steps // groups), operands + tuple(riders)
        body = _attn_b_decode_kernel
    return pl.pallas_call(
        body,
        grid=(batch, KV_CB, nch),
        in_specs=in_specs,
        out_specs=out_specs,
        out_shape=out_shape,
        scratch_shapes=[pltpu.VMEM((2, BLK + B_CHUNK, LANES), BF16),
                        pltpu.VMEM((2, BLK + B_CHUNK, LANES), BF16),
                        pltpu.VMEM((2, Q_PER_KV, 2 * BLK, 2 * BLK), F32),
                        pltpu.VMEM((2, 2 * BLK, LANES), BF16)],
        compiler_params=_cparams(("arbitrary", "arbitrary", "arbitrary")),
        name="attn_b",
    )(*operands)


def _decode_kernel(zt_ref, ka_ref, va_ref, kb_ref, vb_ref, bias_a_ref, bias_b_ref, sink_ref, sel_ref, ones_ref,
                   ya_ref, yb_ref):
    n_h, n_kv = ya_ref.shape[-1], kb_ref.shape[0]
    zt = zt_ref[...]
    split_at = dict(zip(("qa", "ka", "va", "ga", "qb", "kb", "vb", "gb"),
                        (0, n_h, 2 * n_h, 3 * n_h, 4 * n_h, 5 * n_h, 5 * n_h + n_kv, 5 * n_h + 2 * n_kv)))
    cols = lambda name: zt[:, split_at[name]:split_at[name] + n_h]
    eye = lax.broadcasted_iota(jnp.int32, (n_h, n_h), 0) == lax.broadcasted_iota(jnp.int32, (n_h, n_h), 1)
    to_col = lambda row: jnp.sum(jnp.where(eye, jnp.broadcast_to(row, (n_h, n_h)), 0.0), axis=1, keepdims=True)
    to_row = lambda c: jnp.sum(jnp.where(eye, jnp.broadcast_to(c, (n_h, n_h)), 0.0), axis=0, keepdims=True)

    def mixer(q, k_ref, v_ref, kv_head, bias, k_new, v_new, self_bias, sink):
        t = k_ref.shape[-1]
        parts = [jnp.sum((k_ref[kv_head(h)] * q[:, h:h + 1]).reshape(HEAD_DIM // SUBLANES, SUBLANES, t), axis=0)
                 for h in range(n_h)]
        s = jnp.dot(sel_ref[...], jnp.concatenate(parts, axis=0).astype(BF16),
                    preferred_element_type=F32) + bias
        s_self = jnp.sum(q * k_new, axis=0, keepdims=True) + self_bias
        m = jnp.maximum(to_row(jnp.max(s, axis=1, keepdims=True)), s_self)
        if sink is not None:
            m = jnp.maximum(m, sink)
        p = jnp.exp2(s - to_col(m))
        p_self = jnp.exp2(s_self - m)
        den = to_row(jnp.sum(p, axis=1, keepdims=True)) + p_self
        if sink is not None:
            den = den + jnp.exp2(sink - m)
        sums = []
        for h in range(n_h):
            w = v_ref[kv_head(h)] * p[h:h + 1, :]
            acc = w[:, :LANES]
            for j in range(1, t // LANES):
                acc = acc + w[:, j * LANES:(j + 1) * LANES]
            sums.append(acc)
        o = jnp.dot(jnp.concatenate(sums, axis=1).astype(BF16), ones_ref[...], preferred_element_type=F32)
        return (o + v_new * p_self) / den

    oa = mixer(cols("qa"), ka_ref, va_ref, lambda h: h, bias_a_ref[...], cols("ka"), cols("va"),
               math.log2(len(DILATIONS)), None)
    ya_ref[...] = (oa * _silu(cols("ga"))).astype(BF16)

    kv_of_lane = lax.broadcasted_iota(jnp.int32, (HEAD_DIM, n_h), 1) // Q_PER_KV
    per_query = lambda name: sum(jnp.where(kv_of_lane == g, zt[:, split_at[name] + g:split_at[name] + g + 1], 0.0)
                                 for g in range(n_kv))
    ob = mixer(cols("qb"), kb_ref, vb_ref, lambda h: h // Q_PER_KV, bias_b_ref[...], per_query("kb"),
               per_query("vb"), 0.0, sink_ref[...])
    yb_ref[...] = (ob * _silu(cols("gb"))).astype(BF16)


def _decode_operands(z_s, cache_a_k, cache_a_v, cache_b_k, cache_b_v, sinks, groups):
    n = z_s.shape[0]
    assert cache_a_k.shape[1] == A_REACH and cache_b_k.shape[1] == B_WINDOW == N_KEYS
    n_h, n_kv = A_HEADS // groups, B_KV_HEADS // groups
    assert A_HEADS == B_HEADS and n_h * groups == A_HEADS and n_kv * groups == B_KV_HEADS
    slopes = lambda nh: jnp.exp2(-8.0 * jnp.arange(1, nh + 1, dtype=F32) / nh)
    dd = A_REACH - jnp.arange(A_REACH)
    mult = sum(((dd % d == 0) & (dd <= w)).astype(F32) for w, d in DILATIONS)
    bias_a = jnp.where(mult > 0, -slopes(A_HEADS)[:, None] * dd.astype(F32) * LOG2E
                       + jnp.log2(jnp.maximum(mult, 1.0)), NEG)
    bias_b = -slopes(B_HEADS)[:, None] * (B_WINDOW - jnp.arange(B_WINDOW)).astype(F32) * LOG2E
    by_head = lambda c: c.transpose(0, 2, 3, 1)
    sink2 = (sinks * LOG2E).reshape(groups, 1, n_h)
    heads = z_s.reshape(n, D_IN // HEAD_DIM, HEAD_DIM)
    share = lambda g: jnp.concatenate(
        [heads[:, off // HEAD_DIM + g * w:off // HEAD_DIM + (g + 1) * w]
         for off, w in zip((QA0, KA0, VA0, GA0, QB0, KB0, VB0, GB0), (n_h,) * 5 + (n_kv,) * 2 + (n_h,))], axis=1)
    zt = jnp.stack([share(g) for g in range(groups)], axis=1).transpose(0, 1, 3, 2)
    sel = (jnp.arange(n_h * SUBLANES)[None, :] // SUBLANES == jnp.arange(n_h)[:, None]).astype(BF16)
    ones = (jnp.arange(n_h * LANES)[:, None] // LANES == jnp.arange(n_h)[None, :]).astype(BF16)
    return (zt, by_head(cache_a_k), by_head(cache_a_v), by_head(cache_b_k), by_head(cache_b_v),
            bias_a, bias_b, sink2, sel, ones)


def _decode_specs(operands, where):
    zt, ka, va, kb, vb, bias_a, bias_b, sink2, sel, ones = operands
    groups, n_h, n_kv = sink2.shape[0], sink2.shape[2], kb.shape[1] // sink2.shape[0]
    at = lambda pick: (lambda *g: pick(*where(*g)))
    buf_spec = lambda a, heads: pl.BlockSpec((None, heads) + a.shape[2:], at(lambda s, grp: (s, grp, 0, 0)))
    rows_spec = lambda a: pl.BlockSpec((n_h, a.shape[1]), at(lambda s, grp: (grp, 0)))
    const = lambda a: pl.BlockSpec(a.shape, lambda *g: (0,) * a.ndim)
    in_specs = [pl.BlockSpec((None, None) + zt.shape[2:], at(lambda s, grp: (s, grp, 0, 0))),
                buf_spec(ka, n_h), buf_spec(va, n_h), buf_spec(kb, n_kv), buf_spec(vb, n_kv),
                rows_spec(bias_a), rows_spec(bias_b),
                pl.BlockSpec((None, 1, n_h), at(lambda s, grp: (grp, 0, 0))), const(sel), const(ones)]
    out_shape = lambda n: [jax.ShapeDtypeStruct((n, groups, HEAD_DIM, n_h), BF16)] * 2
    return in_specs, out_shape


def _decode(operands, first, count):
    groups = operands[7].shape[0]
    n_h = operands[7].shape[2]
    in_specs, out_shape = _decode_specs(operands, lambda i: (first + i // groups, i % groups))
    out_spec = pl.BlockSpec((None, None, HEAD_DIM, n_h), lambda i: (i // groups, i % groups, 0, 0))
    return pl.pallas_call(
        _decode_kernel,
        grid=(count * groups,),
        in_specs=in_specs,
        out_specs=[out_spec, out_spec],
        out_shape=out_shape(count),
        compiler_params=_cparams(("parallel",)),
        name="decode",
    )(*operands)


def _proj_decode_kernel(x_ref, g_ref, w_ref, *refs):
    *dec_in, z_ref, ya_ref, yb_ref, hn_ref = refs

    @pl.when(pl.program_id(1) == 0)
    def _():
        hn_ref[...] = _rms(x_ref[...], g_ref[...]).astype(BF16)

    _proj_tile(hn_ref[...], w_ref, z_ref)
    _decode_kernel(*dec_in, ya_ref, yb_ref)


def _proj_decode(x, g, w_bf, operands, tm):
    rows, d = x.shape
    n_col = D_IN // PROJ_TN
    steps = rows // tm * n_col
    groups, n_h = operands[7].shape[0], operands[7].shape[2]
    assert groups == 1 and steps <= operands[0].shape[0]
    in_specs, out_shape = _decode_specs(operands, lambda i, j: (i * n_col + j, 0))
    y_spec = pl.BlockSpec((None, None, HEAD_DIM, n_h), lambda i, j: (i * n_col + j, 0, 0, 0))
    return pl.pallas_call(
        _proj_decode_kernel,
        grid=(rows // tm, n_col),
        in_specs=[pl.BlockSpec((tm, d), lambda i, j: (i, 0), pipeline_mode=pl.Buffered(1)),
                  pl.BlockSpec((1, d), lambda i, j: (0, 0)),
                  pl.BlockSpec((d, PROJ_TN), lambda i, j: (0, j))] + in_specs,
        out_specs=[pl.BlockSpec((PROJ_TN // LANES, tm, LANES), lambda i, j: (j, i, 0)), y_spec, y_spec],
        out_shape=[jax.ShapeDtypeStruct((N_CB, rows, LANES), F32)] + out_shape(steps),
        scratch_shapes=[pltpu.VMEM((tm, d), BF16)],
        compiler_params=_cparams(("parallel", "arbitrary")),
        name="proj_decode",
    )(x, g.reshape(1, d), w_bf, *operands)


def _out_kernel(x_ref, ya_ref, yb_ref, p_ref, wout_ref, gple_ref, wpg_ref, wple_ref, gfin_ref, o_ref):
    y = jnp.concatenate([ya_ref[c] for c in range(A_CB)] + [yb_ref[c] for c in range(B_CB)], axis=1)
    h = x_ref[...] + jnp.dot(y, wout_ref[...], preferred_element_type=F32)
    gate = jax.nn.sigmoid(jnp.dot(_rms(h, gple_ref[...]).astype(BF16), wpg_ref[...],
                                  preferred_element_type=F32))
    ple = jnp.dot(p_ref[...].astype(BF16), wple_ref[...], preferred_element_type=F32)
    o_ref[...] = _rms(h + ple * gate, gfin_ref[...])


N_OUT_REFS = 10


def _out_decode_kernel(*refs):
    _out_kernel(*refs[:N_OUT_REFS - 1], refs[-3])
    _decode_kernel(*refs[N_OUT_REFS - 1:-3], refs[-2], refs[-1])


def _out(x, ya, yb, p, wout_bf, g_ple, wpg_bf, wple_bf, g_final, tm, riders=None, first=0):
    rows, d = x.shape
    resident = lambda a: pl.BlockSpec(a.shape, lambda i: (0,) * a.ndim, pipeline_mode=pl.Buffered(1))
    g_ple, g_final = g_ple.reshape(1, d), g_final.reshape(1, d)
    in_specs = [pl.BlockSpec((tm, d), lambda i: (i, 0)),
                pl.BlockSpec((A_CB, tm, LANES), lambda i: (0, i, 0)),
                pl.BlockSpec((B_CB, tm, LANES), lambda i: (0, i, 0)),
                pl.BlockSpec((tm, p.shape[1]), lambda i: (i, 0)),
                resident(wout_bf), resident(g_ple), resident(wpg_bf), resident(wple_bf), resident(g_final)]
    out_specs = [pl.BlockSpec((tm, d), lambda i: (i, 0))]
    out_shape = [jax.ShapeDtypeStruct((rows, d), F32)]
    operands = (x, ya, yb, p, wout_bf, g_ple, wpg_bf, wple_bf, g_final)
    assert len(operands) + 1 == N_OUT_REFS
    if riders is None:
        return pl.pallas_call(_out_kernel, grid=(rows // tm,), in_specs=in_specs, out_specs=out_specs[0],
                              out_shape=out_shape[0], compiler_params=_cparams(("parallel",)), name="out")(*operands)
    groups, n_h = riders[7].shape[0], riders[7].shape[2]
    steps = rows // tm
    assert steps % groups == 0 and first + steps // groups <= riders[0].shape[0]
    rider_specs, rider_shape = _decode_specs(riders, lambda i: (first + i // groups, i % groups))
    y_spec = pl.BlockSpec((None, None, HEAD_DIM, n_h), lambda i: (i // groups, i % groups, 0, 0))
    return pl.pallas_call(
        _out_decode_kernel,
        grid=(steps,),
        in_specs=in_specs + rider_specs,
        out_specs=out_specs + [y_spec, y_spec],
        out_shape=out_shape + rider_shape(steps // groups),
        compiler_params=_cparams(("arbitrary",)),
        name="out_decode",
    )(*operands, *riders)


def _col_blocks(y):
    return y.reshape(y.shape[0], y.shape[1] // LANES, LANES).transpose(1, 0, 2)


def kernel(x_prompt, x_sample, cache_a_k, cache_a_v, cache_b_k, cache_b_v, p_prompt, p_sample,
           g_mix, w_in, sinks, w_out, g_ple, w_pg, w_ple, g_final):
    depth = w_in.shape[0]
    batch, seq, d = x_prompt.shape
    n_s, t_s, _ = x_sample.shape
    assert depth == 1 and t_s == 1 and w_in.shape[2] == D_IN
    xp = x_prompt.reshape(batch * seq, d)
    xs = x_sample.reshape(n_s, d)
    i = 0
    w_in_bf, w_out_bf, w_pg_bf, w_ple_bf = (w[i].astype(BF16) for w in (w_in, w_out, w_pg, w_ple))

    z_s = _proj(xs, g_mix[i], w_in_bf, tm=n_s).transpose(1, 0, 2).reshape(n_s, D_IN)
    dec_whole, dec_halves = (_decode_operands(z_s, cache_a_k[i], cache_a_v[i], cache_b_k[i], cache_b_v[i],
                                              sinks[i], groups) for groups in (1, 2))

    z, *y_s0 = _proj_decode(xp, g_mix[i], w_in_bf, dec_whole, tm=1024)
    ya, ak_t, av_t = _attn_a(z, batch, seq)
    n0 = y_s0[0].shape[0]
    yb, bk_t, bv_t, *y_s1 = _attn_b(z, sinks[i], batch, seq, riders=dec_halves, first=n0)
    n1 = n0 + y_s1[0].shape[0]
    y_prompt, *y_s2 = _out(xp, ya, yb, p_prompt[i].reshape(batch * seq, -1), w_out_bf, g_ple[i], w_pg_bf, w_ple_bf,
                           g_final, tm=256, riders=dec_halves, first=n1)
    new_p = [t.transpose(0, 3, 1, 2)[None] for t in (ak_t, av_t, bk_t, bv_t)]

    n2 = n1 + y_s2[0].shape[0]
    rode = [y_s0, y_s1, y_s2] + ([_decode(dec_whole, n2, n_s - n2)] if n2 < n_s else [])
    rows_of = lambda t: t.transpose(0, 1, 3, 2).reshape(t.shape[0], A_WIDTH)
    ya_s, yb_s = (jnp.concatenate([rows_of(t) for t in parts]) for parts in zip(*rode))
    y_sample = _out(xs, _col_blocks(ya_s), _col_blocks(yb_s), p_sample[i].reshape(n_s, -1), w_out_bf, g_ple[i],
                    w_pg_bf, w_ple_bf, g_final, tm=n_s)
    new_s = [z_s[:, off:off + width].reshape(1, n_s, 1, width // HEAD_DIM, HEAD_DIM)
             for off, width in ((KA0, A_WIDTH), (VA0, A_WIDTH), (KB0, B_KV_WIDTH), (VB0, B_KV_WIDTH))]

    return (y_prompt.reshape(batch, seq, d), y_sample.reshape(n_s, t_s, d), *new_p, *new_s)
```

```python
import functools
import math

import jax
import jax.numpy as jnp
from jax import lax
from jax.experimental import pallas as pl
from jax.experimental.pallas import tpu as pltpu

F32 = jnp.float32
BF16 = jnp.bfloat16

LANES = 128
SUBLANES = 8
HEAD_DIM = 64
A_HEADS = 16
B_HEADS = 16
B_KV_HEADS = 4
BLK = 128
N_KEYS = 128
DILATIONS = ((128, 1), (512, 4), (2048, 16))
A_REACH = 2048
B_WINDOW = 128
EPS = 1e-6
NEG = -1e30

A_WIDTH = A_HEADS * HEAD_DIM
B_WIDTH = B_HEADS * HEAD_DIM
B_KV_WIDTH = B_KV_HEADS * HEAD_DIM
IN_SPLITS = (A_WIDTH, A_WIDTH, A_WIDTH, A_WIDTH, B_WIDTH, B_KV_WIDTH, B_KV_WIDTH, B_WIDTH)
D_IN = sum(IN_SPLITS)
N_CB = D_IN // LANES
QA0, KA0, VA0, GA0, QB0, KB0, VB0, GB0 = (sum(IN_SPLITS[:i]) for i in range(8))
A_CB = A_WIDTH // LANES
B_CB = B_WIDTH // LANES
KV_CB = B_KV_WIDTH // LANES
Q_PER_KV = B_HEADS // B_KV_HEADS

PROJ_TN = 512
B_CHUNK = 1024
A_UNITS = 16
B_UNITS = 16
LOG2E = math.log2(math.e)
Q_SCALE = HEAD_DIM ** -0.5 * LOG2E
VMEM_LIMIT = 56 * 1024 * 1024


def _cparams(sem):
    return pltpu.CompilerParams(dimension_semantics=sem, vmem_limit_bytes=VMEM_LIMIT)


def _rms(x, g):
    return x * lax.rsqrt(jnp.mean(x * x, axis=-1, keepdims=True) + EPS) * g


def _silu(g):
    return g * jax.nn.sigmoid(g)


def _proj_tile(hn, w_ref, z_ref):
    acc = jnp.dot(hn, w_ref[...], preferred_element_type=F32)
    col = pl.program_id(1) * PROJ_TN
    is_q = ((col >= QA0) & (col < QA0 + A_WIDTH)) | ((col >= QB0) & (col < QB0 + B_WIDTH))
    acc = acc * jnp.where(is_q, Q_SCALE, 1.0)
    for c in range(PROJ_TN // LANES):
        z_ref[c] = acc[:, c * LANES:(c + 1) * LANES]


def _proj_kernel(x_ref, g_ref, w_ref, z_ref, wb_ref, hn_ref):
    @pl.when(pl.program_id(1) == 0)
    def _():
        hn_ref[...] = _rms(x_ref[...], g_ref[...]).astype(BF16)

    wb_ref[...] = w_ref[...].astype(BF16)
    _proj_tile(hn_ref[...], wb_ref, z_ref)


def _proj(x, g, w):
    rows, d = x.shape
    w_spec = pl.BlockSpec((d, PROJ_TN), lambda i, j: (0, j))
    return pl.pallas_call(
        _proj_kernel,
        grid=(1, D_IN // PROJ_TN),
        in_specs=[pl.BlockSpec((rows, d), lambda i, j: (0, 0)), pl.BlockSpec((1, d), lambda i, j: (0, 0)), w_spec],
        out_specs=[pl.BlockSpec((PROJ_TN // LANES, rows, LANES), lambda i, j: (j, 0, 0)), w_spec],
        out_shape=[jax.ShapeDtypeStruct((N_CB, rows, LANES), F32), jax.ShapeDtypeStruct(w.shape, BF16)],
        scratch_shapes=[pltpu.VMEM((rows, d), BF16)],
        compiler_params=_cparams(("arbitrary", "arbitrary")),
        name="proj",
    )(x, g.reshape(1, d), w)


def _band_bias(slope, step):
    qi = lax.broadcasted_iota(jnp.int32, (BLK, 2 * BLK), 0)
    kj = lax.broadcasted_iota(jnp.int32, (BLK, 2 * BLK), 1)
    dist = qi + BLK - kj
    valid = (dist >= 0) & (dist <= N_KEYS)
    return jnp.where(valid, -slope * (dist * step).astype(F32) * LOG2E, NEG)


def _head_slope(h, n_heads):
    return jnp.exp2(jnp.full((1, 1), -8.0 / n_heads, F32) * (h + 1).astype(F32))


def _low_lanes(rows):
    return lax.broadcasted_iota(jnp.int32, (rows, LANES), 1) < HEAD_DIM


def _fill_ones(ones_s):
    ones_s[0] = jnp.where(_low_lanes(2 * BLK), 1.0, 0.0).astype(BF16)
    ones_s[1] = jnp.where(_low_lanes(2 * BLK), 0.0, 1.0).astype(BF16)


def _run_blocks(units, load, ones_s, finish, sinks_of=None):
    lo = _low_lanes(BLK)

    def scores(unit):
        q, kk, vv, bias = load(unit)
        zero = jnp.zeros_like(q)
        q2 = jnp.concatenate([jnp.where(lo, q, zero), jnp.where(lo, zero, q)], axis=0)
        return lax.dot_general(q2, kk, (((1,), (1,)), ((), ())), preferred_element_type=F32) + bias, vv

    def attend(s, vv, sinks):
        nk = vv.shape[0]
        halves = (s[:BLK], s[BLK:])
        m = [jnp.max(h, axis=1, keepdims=True) for h in halves]
        if sinks is not None:
            m = [jnp.maximum(mh, sk) for mh, sk in zip(m, sinks)]
        lhs = jnp.concatenate([jnp.exp2(h - mh).astype(BF16) for h, mh in zip(halves, m)], axis=1)
        zero = jnp.zeros_like(vv)
        rhs = jnp.concatenate(
            [jnp.concatenate([jnp.where(_low_lanes(nk), vv, zero), ones_s[0, :nk, :]], axis=1),
             jnp.concatenate([jnp.where(_low_lanes(nk), zero, vv), ones_s[1, :nk, :]], axis=1)], axis=0)
        ol = jnp.dot(lhs, rhs, preferred_element_type=F32)
        mt = jnp.where(lo, jnp.broadcast_to(m[0], (BLK, LANES)), jnp.broadcast_to(m[1], (BLK, LANES)))
        den = ol[:, LANES:]
        if sinks is not None:
            den = den + jnp.exp2(jnp.where(lo, sinks[0], sinks[1]) - mt)
        return ol[:, :LANES], den, mt

    results = []
    nxt = scores(units[0])
    for k, unit in enumerate(units):
        s, vv = nxt
        if k + 1 < len(units):
            nxt = scores(units[k + 1])
        results.append(attend(s, vv, None if sinks_of is None else sinks_of(unit)))
    for unit, res in zip(units, results):
        finish(unit, res)


def _attn_a_kernel(q_ref, k_ref, v_ref, g_ref, y_ref, kt_ref, vt_ref,
                   q1_s, k1_s, v1_s, q4_s, k4_s, v4_s, q16_s, k16_s, v16_s,
                   by4_s, st16_s, st4_s, bias_s, ones_s, *, seq):
    keep = kt_ref.shape[-1]
    for src, dst in ((k_ref, kt_ref), (v_ref, vt_ref)):
        for j in range(keep // LANES):
            tile = src[seq - keep + j * LANES:seq - keep + (j + 1) * LANES, :]
            dst[:, :, j * LANES:(j + 1) * LANES] = tile.T.reshape(2, HEAD_DIM, LANES)

    hp = pl.program_id(1)
    steps = tuple(d for _, d in DILATIONS)

    for pi, step in enumerate(steps):
        for hh in range(2):
            bias_s[pi, hh * BLK:(hh + 1) * BLK, :] = _band_bias(_head_slope(2 * hp + hh, A_HEADS), step)
    _fill_ones(ones_s)

    assert steps == (1, 4, 16)
    for src, dsts in ((q_ref, (q1_s, q4_s, q16_s)), (k_ref, (k1_s, k4_s, k16_s)), (v_ref, (v1_s, v4_s, v16_s))):
        dsts[0][0] = src[...].astype(BF16)
        for r in range(4):
            rows = src[pl.ds(r, seq // 4, stride=4), :]
            by4_s[r] = rows
            dsts[1][r] = rows.astype(BF16)
        for r in range(16):
            dsts[2][r] = by4_s[r % 4, pl.ds(r // 4, seq // 16, stride=4), :].astype(BF16)

    def merge(a, b):
        m = jnp.maximum(a[2], b[2])
        wa = jnp.exp2(a[2] - m)
        wb = jnp.exp2(b[2] - m)
        return wa * a[0] + wb * b[0], wa * a[1] + wb * b[1], m

    def blocks(pi, refs, n_res, finish):
        nb = seq // steps[pi] // BLK
        q_s, k_s, v_s = refs

        def load(unit):
            r, row0, first = unit
            if first:
                keys, bias = pl.ds(row0, BLK), bias_s[pi, :, BLK:]
            else:
                keys, bias = pl.ds(row0 - BLK, 2 * BLK), bias_s[pi]
            return q_s[r, pl.ds(row0, BLK), :], k_s[r, keys, :], v_s[r, keys, :], bias

        def run(units):
            _run_blocks(units, load, ones_s, finish)

        def trips(n, unit_of):
            if n == 0:
                return
            per = max(u for u in range(1, A_UNITS + 1) if n % u == 0)
            if n == per:
                run([unit_of(i) for i in range(n)])
                return

            def trip(i, c):
                run([unit_of(i * per + u) for u in range(per)])
                return c
            lax.fori_loop(0, n // per, trip, 0)

        row = lambda b: b * BLK if isinstance(b, int) else pl.multiple_of(b * BLK, BLK)
        if n_res > 1:
            trips(n_res, lambda idx: (idx, 0, True))
            trips(n_res * (nb - 1), lambda idx: (idx // (nb - 1), row(idx % (nb - 1) + 1), False))
        else:
            run([(0, 0, True)] + [(0, row(b), False) for b in range(1, A_UNITS)])
            trips(nb - A_UNITS, lambda idx: (0, row(idx + A_UNITS), False))

    def finish16(unit, res):
        r, row0, _ = unit
        rows = pl.ds(r // 4 + 4 * row0, BLK, stride=4)
        for c in range(3):
            st16_s[c, r % 4, rows, :] = res[c]
    blocks(2, (q16_s, k16_s, v16_s), 16, finish16)

    def finish4(unit, res):
        r, row0, _ = unit
        res = merge(tuple(st16_s[c, r, pl.ds(row0, BLK), :] for c in range(3)), res)
        rows = pl.ds(r + 4 * row0, BLK, stride=4)
        for c in range(3):
            st4_s[c, rows, :] = res[c]
    blocks(1, (q4_s, k4_s, v4_s), 4, finish4)

    def finish1(unit, res):
        _, row0, _ = unit
        rows = pl.ds(row0, BLK)
        o, l, _ = merge(tuple(st4_s[c, rows, :] for c in range(3)), res)
        y_ref[rows, :] = (o / l * _silu(g_ref[rows, :])).astype(BF16)
    blocks(0, (q1_s, k1_s, v1_s), 1, finish1)


def _attn_a(z, batch, seq):
    assert DILATIONS == ((N_KEYS, 1), (4 * N_KEYS, 4), (16 * N_KEYS, 16)) and N_KEYS == BLK
    assert seq % (16 * BLK) == 0 and seq // 16 >= 2 * BLK and seq // BLK >= A_UNITS
    keep = min(A_REACH, seq)
    spec = lambda off: pl.BlockSpec((None, seq, LANES), lambda b, hp: (off // LANES + hp, b, 0))
    res = lambda d, dt: pltpu.VMEM((d, seq // d, LANES), dt)
    kv_spec = pl.BlockSpec((None, 2, HEAD_DIM, keep), lambda b, hp: (b, hp, 0, 0))
    kv_shape = jax.ShapeDtypeStruct((batch, A_HEADS, HEAD_DIM, keep), F32)
    return pl.pallas_call(
        functools.partial(_attn_a_kernel, seq=seq),
        grid=(batch, A_CB),
        in_specs=[spec(QA0), spec(KA0), spec(VA0), spec(GA0)],
        out_specs=[pl.BlockSpec((None, seq, LANES), lambda b, hp: (hp, b, 0)), kv_spec, kv_spec],
        out_shape=[jax.ShapeDtypeStruct((A_CB, batch * seq, LANES), BF16), kv_shape, kv_shape],
        scratch_shapes=[res(d, BF16) for d in (1, 1, 1, 4, 4, 4, 16, 16, 16)]
                       + [res(4, F32), pltpu.VMEM((3, 4, seq // 4, LANES), F32), pltpu.VMEM((3, seq, LANES), F32),
                          pltpu.VMEM((3, 2 * BLK, 2 * BLK), F32), pltpu.VMEM((2, 2 * BLK, LANES), BF16)],
        compiler_params=_cparams(("parallel", "parallel")),
        name="attn_a",
    )(z, z, z, z)


def _attn_b_kernel(sink_ref, q_ref, kc_ref, kp_ref, vc_ref, vp_ref, g_ref, y_ref, kt_ref, vt_ref,
                   kk_s, vv_s, bias_s, ones_s):
    kvc = pl.program_id(1)
    ch = pl.program_id(2)
    heads = 2 * Q_PER_KV
    q_blocks = heads // 2

    @pl.when(ch == pl.num_programs(2) - 1)
    def _():
        keep = kt_ref.shape[-1]
        for src, dst in ((kc_ref, kt_ref), (vc_ref, vt_ref)):
            dst[...] = src[B_CHUNK - keep:, :].T.reshape(2, HEAD_DIM, keep)

    for cur, prev, dst in ((kc_ref, kp_ref, kk_s), (vc_ref, vp_ref, vv_s)):
        for src, a, n in ((prev, 0, BLK), (cur, BLK, B_CHUNK)):
            x = src[...]
            swapped = pltpu.roll(x, HEAD_DIM, 1)
            dst[0, a:a + n, :] = jnp.where(_low_lanes(n), x, swapped).astype(BF16)
            dst[1, a:a + n, :] = jnp.where(_low_lanes(n), swapped, x).astype(BF16)
    _fill_ones(ones_s)

    in_prev = lax.broadcasted_iota(jnp.int32, (BLK, 2 * BLK), 1) < BLK
    for hl in range(heads):
        full = _band_bias(_head_slope(kvc * heads + hl, B_HEADS), 1)
        rows = slice(hl % 2 * BLK, (hl % 2 + 1) * BLK)
        bias_s[0, hl // 2, rows, :] = full
        bias_s[1, hl // 2, rows, :] = jnp.where(in_prev, NEG, full)
    at_start = (ch == 0).astype(jnp.int32)

    def load(unit):
        i, qc = unit
        g = 2 * qc // Q_PER_KV
        keys = pl.ds(i * BLK, 2 * BLK)
        return (q_ref[qc, pl.ds(i * BLK, BLK), :].astype(BF16), kk_s[g, keys, :], vv_s[g, keys, :],
                bias_s[at_start if i == 0 else 0, qc])

    def sinks_of(unit):
        return tuple(sink_ref[kvc * heads + 2 * unit[1] + hh] * LOG2E for hh in range(2))

    def finish(unit, res):
        i, qc = unit
        rows = pl.ds(i * BLK, BLK)
        y_ref[qc, rows, :] = (res[0] / res[1] * _silu(g_ref[qc, rows, :])).astype(BF16)

    units = [(i, qc) for i in range(B_CHUNK // BLK) for qc in range(q_blocks)]
    for a in range(0, len(units), B_UNITS):
        _run_blocks(units[a:a + B_UNITS], load, ones_s, finish, sinks_of)


N_B_IN, N_B_OUT, N_B_SCRATCH = 7, 3, 4
N_DEC_IN, N_DEC_OUT = 10, 2


def _attn_b_decode_kernel(*refs, n_cast):
    ins, rest = refs[:N_B_IN + N_DEC_IN + n_cast], refs[N_B_IN + N_DEC_IN + n_cast:]
    b_out, dec_out = rest[:N_B_OUT], rest[N_B_OUT:N_B_OUT + N_DEC_OUT]
    cast_out = rest[N_B_OUT + N_DEC_OUT:N_B_OUT + N_DEC_OUT + n_cast]
    for src, dst in zip(ins[N_B_IN + N_DEC_IN:], cast_out):
        dst[...] = src[...].astype(BF16)
    _attn_b_kernel(*ins[:N_B_IN], *b_out, *rest[-N_B_SCRATCH:])
    _decode_kernel(*ins[N_B_IN:N_B_IN + N_DEC_IN], *dec_out)


def _attn_b(z, sinks, batch, seq, riders=None, first=0, to_bf16=()):
    assert seq % B_CHUNK == 0 and B_WINDOW == BLK
    nch = seq // B_CHUNK
    keep = min(B_WINDOW, seq)
    kv_spec = pl.BlockSpec((None, 2, HEAD_DIM, keep), lambda b, kvc, c: (b, kvc, 0, 0))
    kv_shape = jax.ShapeDtypeStruct((batch, B_KV_HEADS, HEAD_DIM, keep), F32)
    qpk = B_CB // KV_CB
    row = lambda b, c: b * nch + c
    prev = lambda b, c: jnp.maximum((b * nch + c) * (B_CHUNK // BLK) - 1, 0)
    cur_spec = lambda off: pl.BlockSpec((None, B_CHUNK, LANES),
                                        lambda b, kvc, c: (off // LANES + kvc, row(b, c), 0))
    prev_spec = lambda off: pl.BlockSpec((None, BLK, LANES),
                                         lambda b, kvc, c: (off // LANES + kvc, prev(b, c), 0))
    wide = lambda off: pl.BlockSpec((qpk, B_CHUNK, LANES),
                                    lambda b, kvc, c: (off // LANES // qpk + kvc, row(b, c), 0))
    in_specs = [pl.BlockSpec(memory_space=pltpu.SMEM),
                wide(QB0), cur_spec(KB0), prev_spec(KB0), cur_spec(VB0), prev_spec(VB0), wide(GB0)]
    out_specs = [pl.BlockSpec((qpk, B_CHUNK, LANES), lambda b, kvc, c: (kvc, row(b, c), 0)), kv_spec, kv_spec]
    out_shape = [jax.ShapeDtypeStruct((B_CB, batch * seq, LANES), BF16), kv_shape, kv_shape]
    operands = (sinks, z, z, z, z, z, z)
    assert len(in_specs) == N_B_IN and len(out_specs) == N_B_OUT
    body = _attn_b_kernel
    if riders is not None:
        groups, n_h = riders[7].shape[0], riders[7].shape[2]
        steps = batch * KV_CB * nch
        step = lambda b, kvc, c: (b * KV_CB + kvc) * nch + c
        assert steps % groups == 0 and first + steps // groups <= riders[0].shape[0]
        rider_specs, rider_shape = _decode_specs(
            riders, lambda b, kvc, c: (first + step(b, kvc, c) // groups, step(b, kvc, c) % groups))
        y_spec = pl.BlockSpec((None, None, HEAD_DIM, n_h),
                              lambda b, kvc, c: (step(b, kvc, c) // groups, step(b, kvc, c) % groups, 0, 0))
        assert all(w.shape[0] % (steps * 2 * SUBLANES) == 0 for w in to_bf16)
        cast_specs = [pl.BlockSpec((w.shape[0] // steps, w.shape[1]), lambda b, kvc, c: (step(b, kvc, c), 0))
                      for w in to_bf16]
        in_specs = in_specs + rider_specs + cast_specs
        out_specs = out_specs + [y_spec, y_spec] + cast_specs
        out_shape = out_shape + rider_shape(steps // groups) + [jax.ShapeDtypeStruct(w.shape, BF16) for w in to_bf16]
        operands = operands + tuple(riders) + tuple(to_bf16)
        assert len(rider_specs) == N_DEC_IN
        body = functools.partial(_attn_b_decode_kernel, n_cast=len(to_bf16))
    return pl.pallas_call(
        body,
        grid=(batch, KV_CB, nch),
        in_specs=in_specs,
        out_specs=out_specs,
        out_shape=out_shape,
        scratch_shapes=[pltpu.VMEM((2, BLK + B_CHUNK, LANES), BF16),
                        pltpu.VMEM((2, BLK + B_CHUNK, LANES), BF16),
                        pltpu.VMEM((2, Q_PER_KV, 2 * BLK, 2 * BLK), F32),
                        pltpu.VMEM((2, 2 * BLK, LANES), BF16)],
        compiler_params=_cparams(("arbitrary", "arbitrary", "arbitrary")),
        name="attn_b",
    )(*operands)


def _decode_kernel(zt_ref, ka_ref, va_ref, kb_ref, vb_ref, bias_a_ref, bias_b_ref, sink_ref, sel_ref, ones_ref,
                   ya_ref, yb_ref):
    n_h, n_kv = ya_ref.shape[-1], kb_ref.shape[0]
    zt = zt_ref[...]
    split_at = dict(zip(("qa", "ka", "va", "ga", "qb", "kb", "vb", "gb"),
                        (0, n_h, 2 * n_h, 3 * n_h, 4 * n_h, 5 * n_h, 5 * n_h + n_kv, 5 * n_h + 2 * n_kv)))
    cols = lambda name: zt[:, split_at[name]:split_at[name] + n_h]
    eye = lax.broadcasted_iota(jnp.int32, (n_h, n_h), 0) == lax.broadcasted_iota(jnp.int32, (n_h, n_h), 1)
    to_col = lambda row: jnp.sum(jnp.where(eye, jnp.broadcast_to(row, (n_h, n_h)), 0.0), axis=1, keepdims=True)
    to_row = lambda c: jnp.sum(jnp.where(eye, jnp.broadcast_to(c, (n_h, n_h)), 0.0), axis=0, keepdims=True)

    def mixer(q, k_ref, v_ref, kv_head, bias, k_new, v_new, self_bias, sink):
        t = k_ref.shape[-1]
        parts = [jnp.sum((k_ref[kv_head(h)] * q[:, h:h + 1]).reshape(HEAD_DIM // SUBLANES, SUBLANES, t), axis=0)
                 for h in range(n_h)]
        s = jnp.dot(sel_ref[...], jnp.concatenate(parts, axis=0).astype(BF16),
                    preferred_element_type=F32) + bias
        s_self = jnp.sum(q * k_new, axis=0, keepdims=True) + self_bias
        m = jnp.maximum(to_row(jnp.max(s, axis=1, keepdims=True)), s_self)
        if sink is not None:
            m = jnp.maximum(m, sink)
        p = jnp.exp2(s - to_col(m))
        p_self = jnp.exp2(s_self - m)
        den = to_row(jnp.sum(p, axis=1, keepdims=True)) + p_self
        if sink is not None:
            den = den + jnp.exp2(sink - m)
        sums = []
        for h in range(n_h):
            w = v_ref[kv_head(h)] * p[h:h + 1, :]
            acc = w[:, :LANES]
            for j in range(1, t // LANES):
                acc = acc + w[:, j * LANES:(j + 1) * LANES]
            sums.append(acc)
        o = jnp.dot(jnp.concatenate(sums, axis=1).astype(BF16), ones_ref[...], preferred_element_type=F32)
        return (o + v_new * p_self) / den

    oa = mixer(cols("qa"), ka_ref, va_ref, lambda h: h, bias_a_ref[...], cols("ka"), cols("va"),
               math.log2(len(DILATIONS)), None)
    ya_ref[...] = (oa * _silu(cols("ga"))).astype(BF16)

    kv_of_lane = lax.broadcasted_iota(jnp.int32, (HEAD_DIM, n_h), 1) // Q_PER_KV
    per_query = lambda name: sum(jnp.where(kv_of_lane == g, zt[:, split_at[name] + g:split_at[name] + g + 1], 0.0)
                                 for g in range(n_kv))
    ob = mixer(cols("qb"), kb_ref, vb_ref, lambda h: h // Q_PER_KV, bias_b_ref[...], per_query("kb"),
               per_query("vb"), 0.0, sink_ref[...])
    yb_ref[...] = (ob * _silu(cols("gb"))).astype(BF16)


def _decode_operands(z_s, cache_a_k, cache_a_v, cache_b_k, cache_b_v, sinks, groups):
    n = z_s.shape[0]
    assert cache_a_k.shape[1] == A_REACH and cache_b_k.shape[1] == B_WINDOW == N_KEYS
    n_h, n_kv = A_HEADS // groups, B_KV_HEADS // groups
    assert A_HEADS == B_HEADS and n_h * groups == A_HEADS and n_kv * groups == B_KV_HEADS
    slopes = lambda nh: jnp.exp2(-8.0 * jnp.arange(1, nh + 1, dtype=F32) / nh)
    dd = A_REACH - jnp.arange(A_REACH)
    mult = sum(((dd % d == 0) & (dd <= w)).astype(F32) for w, d in DILATIONS)
    bias_a = jnp.where(mult > 0, -slopes(A_HEADS)[:, None] * dd.astype(F32) * LOG2E
                       + jnp.log2(jnp.maximum(mult, 1.0)), NEG)
    bias_b = -slopes(B_HEADS)[:, None] * (B_WINDOW - jnp.arange(B_WINDOW)).astype(F32) * LOG2E
    by_head = lambda c: c.transpose(0, 2, 3, 1)
    sink2 = (sinks * LOG2E).reshape(groups, 1, n_h)
    heads = z_s.reshape(n, D_IN // HEAD_DIM, HEAD_DIM)
    share = lambda g: jnp.concatenate(
        [heads[:, off // HEAD_DIM + g * w:off // HEAD_DIM + (g + 1) * w]
         for off, w in zip((QA0, KA0, VA0, GA0, QB0, KB0, VB0, GB0), (n_h,) * 5 + (n_kv,) * 2 + (n_h,))], axis=1)
    zt = jnp.stack([share(g) for g in range(groups)], axis=1).transpose(0, 1, 3, 2)
    sel = (jnp.arange(n_h * SUBLANES)[None, :] // SUBLANES == jnp.arange(n_h)[:, None]).astype(BF16)
    ones = (jnp.arange(n_h * LANES)[:, None] // LANES == jnp.arange(n_h)[None, :]).astype(BF16)
    return (zt, by_head(cache_a_k), by_head(cache_a_v), by_head(cache_b_k), by_head(cache_b_v),
            bias_a, bias_b, sink2, sel, ones)


def _decode_specs(operands, where):
    zt, ka, va, kb, vb, bias_a, bias_b, sink2, sel, ones = operands
    groups, n_h, n_kv = sink2.shape[0], sink2.shape[2], kb.shape[1] // sink2.shape[0]
    at = lambda pick: (lambda *g: pick(*where(*g)))
    buf_spec = lambda a, heads: pl.BlockSpec((None, heads) + a.shape[2:], at(lambda s, grp: (s, grp, 0, 0)))
    rows_spec = lambda a: pl.BlockSpec((n_h, a.shape[1]), at(lambda s, grp: (grp, 0)))
    const = lambda a: pl.BlockSpec(a.shape, lambda *g: (0,) * a.ndim)
    in_specs = [pl.BlockSpec((None, None) + zt.shape[2:], at(lambda s, grp: (s, grp, 0, 0))),
                buf_spec(ka, n_h), buf_spec(va, n_h), buf_spec(kb, n_kv), buf_spec(vb, n_kv),
                rows_spec(bias_a), rows_spec(bias_b),
                pl.BlockSpec((None, 1, n_h), at(lambda s, grp: (grp, 0, 0))), const(sel), const(ones)]
    out_shape = lambda n: [jax.ShapeDtypeStruct((n, groups, HEAD_DIM, n_h), BF16)] * 2
    return in_specs, out_shape


def _decode(operands, first, count):
    groups = operands[7].shape[0]
    n_h = operands[7].shape[2]
    in_specs, out_shape = _decode_specs(operands, lambda i: (first + i // groups, i % groups))
    out_spec = pl.BlockSpec((None, None, HEAD_DIM, n_h), lambda i: (i // groups, i % groups, 0, 0))
    return pl.pallas_call(
        _decode_kernel,
        grid=(count * groups,),
        in_specs=in_specs,
        out_specs=[out_spec, out_spec],
        out_shape=out_shape(count),
        compiler_params=_cparams(("parallel",)),
        name="decode",
    )(*operands)


def _proj_decode_kernel(x_ref, g_ref, w_ref, *refs):
    *dec_in, z_ref, ya_ref, yb_ref, hn_ref = refs

    @pl.when(pl.program_id(1) == 0)
    def _():
        hn_ref[...] = _rms(x_ref[...], g_ref[...]).astype(BF16)

    _proj_tile(hn_ref[...], w_ref, z_ref)
    _decode_kernel(*dec_in, ya_ref, yb_ref)


def _proj_decode(x, g, w_bf, operands, tm):
    rows, d = x.shape
    n_col = D_IN // PROJ_TN
    steps = rows // tm * n_col
    groups, n_h = operands[7].shape[0], operands[7].shape[2]
    assert groups == 1 and steps <= operands[0].shape[0]
    in_specs, out_shape = _decode_specs(operands, lambda i, j: (i * n_col + j, 0))
    y_spec = pl.BlockSpec((None, None, HEAD_DIM, n_h), lambda i, j: (i * n_col + j, 0, 0, 0))
    return pl.pallas_call(
        _proj_decode_kernel,
        grid=(rows // tm, n_col),
        in_specs=[pl.BlockSpec((tm, d), lambda i, j: (i, 0), pipeline_mode=pl.Buffered(1)),
                  pl.BlockSpec((1, d), lambda i, j: (0, 0)),
                  pl.BlockSpec((d, PROJ_TN), lambda i, j: (0, j))] + in_specs,
        out_specs=[pl.BlockSpec((PROJ_TN // LANES, tm, LANES), lambda i, j: (j, i, 0)), y_spec, y_spec],
        out_shape=[jax.ShapeDtypeStruct((N_CB, rows, LANES), F32)] + out_shape(steps),
        scratch_shapes=[pltpu.VMEM((tm, d), BF16)],
        compiler_params=_cparams(("parallel", "arbitrary")),
        name="proj_decode",
    )(x, g.reshape(1, d), w_bf, *operands)


def _out_kernel(x_ref, ya_ref, yb_ref, p_ref, wout_ref, gple_ref, wpg_ref, wple_ref, gfin_ref, o_ref):
    y = jnp.concatenate([ya_ref[c] for c in range(A_CB)] + [yb_ref[c] for c in range(B_CB)], axis=1)
    h = x_ref[...] + jnp.dot(y, wout_ref[...], preferred_element_type=F32)
    gate = jax.nn.sigmoid(jnp.dot(_rms(h, gple_ref[...]).astype(BF16), wpg_ref[...],
                                  preferred_element_type=F32))
    ple = jnp.dot(p_ref[...].astype(BF16), wple_ref[...], preferred_element_type=F32)
    o_ref[...] = _rms(h + ple * gate, gfin_ref[...])


N_OUT_REFS = 10


def _out_decode_kernel(*refs):
    _out_kernel(*refs[:N_OUT_REFS - 1], refs[-3])
    _decode_kernel(*refs[N_OUT_REFS - 1:-3], refs[-2], refs[-1])


def _out(x, ya, yb, p, wout_bf, g_ple, wpg_bf, wple_bf, g_final, tm, riders=None, first=0):
    rows, d = x.shape
    resident = lambda a: pl.BlockSpec(a.shape, lambda i: (0,) * a.ndim, pipeline_mode=pl.Buffered(1))
    g_ple, g_final = g_ple.reshape(1, d), g_final.reshape(1, d)
    in_specs = [pl.BlockSpec((tm, d), lambda i: (i, 0)),
                pl.BlockSpec((A_CB, tm, LANES), lambda i: (0, i, 0)),
                pl.BlockSpec((B_CB, tm, LANES), lambda i: (0, i, 0)),
                pl.BlockSpec((tm, p.shape[1]), lambda i: (i, 0)),
                resident(wout_bf), resident(g_ple), resident(wpg_bf), resident(wple_bf), resident(g_final)]
    out_specs = [pl.BlockSpec((tm, d), lambda i: (i, 0))]
    out_shape = [jax.ShapeDtypeStruct((rows, d), F32)]
    operands = (x, ya, yb, p, wout_bf, g_ple, wpg_bf, wple_bf, g_final)
    assert len(operands) + 1 == N_OUT_REFS
    if riders is None:
        return pl.pallas_call(_out_kernel, grid=(rows // tm,), in_specs=in_specs, out_specs=out_specs[0],
                              out_shape=out_shape[0], compiler_params=_cparams(("parallel",)), name="out")(*operands)
    groups, n_h = riders[7].shape[0], riders[7].shape[2]
    steps = rows // tm
    assert steps % groups == 0 and first + steps // groups <= riders[0].shape[0]
    rider_specs, rider_shape = _decode_specs(riders, lambda i: (first + i // groups, i % groups))
    y_spec = pl.BlockSpec((None, None, HEAD_DIM, n_h), lambda i: (i // groups, i % groups, 0, 0))
    return pl.pallas_call(
        _out_decode_kernel,
        grid=(steps,),
        in_specs=in_specs + rider_specs,
        out_specs=out_specs + [y_spec, y_spec],
        out_shape=out_shape + rider_shape(steps // groups),
        compiler_params=_cparams(("arbitrary",)),
        name="out_decode",
    )(*operands, *riders)


def _col_blocks(y):
    return y.reshape(y.shape[0], y.shape[1] // LANES, LANES).transpose(1, 0, 2)


def kernel(x_prompt, x_sample, cache_a_k, cache_a_v, cache_b_k, cache_b_v, p_prompt, p_sample,
           g_mix, w_in, sinks, w_out, g_ple, w_pg, w_ple, g_final):
    depth = w_in.shape[0]
    batch, seq, d = x_prompt.shape
    n_s, t_s, _ = x_sample.shape
    assert depth == 1 and t_s == 1 and w_in.shape[2] == D_IN
    xp = x_prompt.reshape(batch * seq, d)
    xs = x_sample.reshape(n_s, d)
    i = 0
    z_s, w_in_bf = _proj(xs, g_mix[i], w_in[i])
    z_s = z_s.transpose(1, 0, 2).reshape(n_s, D_IN)
    dec_whole, dec_halves = (_decode_operands(z_s, cache_a_k[i], cache_a_v[i], cache_b_k[i], cache_b_v[i],
                                              sinks[i], groups) for groups in (1, 2))

    z, *y_s0 = _proj_decode(xp, g_mix[i], w_in_bf, dec_whole, tm=1024)
    ya, ak_t, av_t = _attn_a(z, batch, seq)
    n0 = y_s0[0].shape[0]
    yb, bk_t, bv_t, *rode_b = _attn_b(z, sinks[i], batch, seq, riders=dec_halves, first=n0,
                                      to_bf16=(w_out[i], w_pg[i], w_ple[i]))
    y_s1, (w_out_bf, w_pg_bf, w_ple_bf) = rode_b[:N_DEC_OUT], rode_b[N_DEC_OUT:]
    n1 = n0 + y_s1[0].shape[0]
    y_prompt, *y_s2 = _out(xp, ya, yb, p_prompt[i].reshape(batch * seq, -1), w_out_bf, g_ple[i], w_pg_bf, w_ple_bf,
                           g_final, tm=256, riders=dec_halves, first=n1)
    new_p = [t.transpose(0, 3, 1, 2)[None] for t in (ak_t, av_t, bk_t, bv_t)]

    n2 = n1 + y_s2[0].shape[0]
    rode = [y_s0, y_s1, y_s2] + ([_decode(dec_whole, n2, n_s - n2)] if n2 < n_s else [])
    rows_of = lambda t: t.transpose(0, 1, 3, 2).reshape(t.shape[0], A_WIDTH)
    ya_s, yb_s = (jnp.concatenate([rows_of(t) for t in parts]) for parts in zip(*rode))
    y_sample = _out(xs, _col_blocks(ya_s), _col_blocks(yb_s), p_sample[i].reshape(n_s, -1), w_out_bf, g_ple[i],
                    w_pg_bf, w_ple_bf, g_final, tm=n_s)
    new_s = [z_s[:, off:off + width].reshape(1, n_s, 1, width // HEAD_DIM, HEAD_DIM)
             for off, width in ((KA0, A_WIDTH), (VA0, A_WIDTH), (KB0, B_KV_WIDTH), (VB0, B_KV_WIDTH))]

    return (y_prompt.reshape(batch, seq, d), y_sample.reshape(n_s, t_s, d), *new_p, *new_s)
```

```python
import functools
import math

import jax
import jax.numpy as jnp
from jax import lax
from jax.experimental import pallas as pl
from jax.experimental.pallas import tpu as pltpu

F32 = jnp.float32
BF16 = jnp.bfloat16

LANES = 128
SUBLANES = 8
HEAD_DIM = 64
A_HEADS = 16
B_HEADS = 16
B_KV_HEADS = 4
BLK = 128
N_KEYS = 128
DILATIONS = ((128, 1), (512, 4), (2048, 16))
A_REACH = 2048
B_WINDOW = 128
EPS = 1e-6
NEG = -1e30

A_WIDTH = A_HEADS * HEAD_DIM
B_WIDTH = B_HEADS * HEAD_DIM
B_KV_WIDTH = B_KV_HEADS * HEAD_DIM
IN_SPLITS = (A_WIDTH, A_WIDTH, A_WIDTH, A_WIDTH, B_WIDTH, B_KV_WIDTH, B_KV_WIDTH, B_WIDTH)
D_IN = sum(IN_SPLITS)
N_CB = D_IN // LANES
QA0, KA0, VA0, GA0, QB0, KB0, VB0, GB0 = (sum(IN_SPLITS[:i]) for i in range(8))
A_CB = A_WIDTH // LANES
B_CB = B_WIDTH // LANES
KV_CB = B_KV_WIDTH // LANES
Q_PER_KV = B_HEADS // B_KV_HEADS

PROJ_TN = 512
B_CHUNK = 1024
A_UNITS = 32
B_UNITS = 16
LOG2E = math.log2(math.e)
Q_SCALE = HEAD_DIM ** -0.5 * LOG2E
VMEM_LIMIT = 56 * 1024 * 1024


def _cparams(sem):
    return pltpu.CompilerParams(dimension_semantics=sem, vmem_limit_bytes=VMEM_LIMIT)


def _rms(x, g):
    return x * lax.rsqrt(jnp.mean(x * x, axis=-1, keepdims=True) + EPS) * g


def _silu(g):
    return g * jax.nn.sigmoid(g)


def _proj_tile(hn, w_ref, z_ref):
    acc = jnp.dot(hn, w_ref[...], preferred_element_type=F32)
    col = pl.program_id(1) * PROJ_TN
    is_q = ((col >= QA0) & (col < QA0 + A_WIDTH)) | ((col >= QB0) & (col < QB0 + B_WIDTH))
    acc = acc * jnp.where(is_q, Q_SCALE, 1.0)
    for c in range(PROJ_TN // LANES):
        z_ref[c] = acc[:, c * LANES:(c + 1) * LANES]


def _proj_kernel(x_ref, g_ref, w_ref, z_ref, wb_ref, hn_ref):
    @pl.when(pl.program_id(1) == 0)
    def _():
        hn_ref[...] = _rms(x_ref[...], g_ref[...]).astype(BF16)

    wb_ref[...] = w_ref[...].astype(BF16)
    _proj_tile(hn_ref[...], wb_ref, z_ref)


def _proj(x, g, w):
    rows, d = x.shape
    w_spec = pl.BlockSpec((d, PROJ_TN), lambda i, j: (0, j))
    return pl.pallas_call(
        _proj_kernel,
        grid=(1, D_IN // PROJ_TN),
        in_specs=[pl.BlockSpec((rows, d), lambda i, j: (0, 0)), pl.BlockSpec((1, d), lambda i, j: (0, 0)), w_spec],
        out_specs=[pl.BlockSpec((PROJ_TN // LANES, rows, LANES), lambda i, j: (j, 0, 0)), w_spec],
        out_shape=[jax.ShapeDtypeStruct((N_CB, rows, LANES), F32), jax.ShapeDtypeStruct(w.shape, BF16)],
        scratch_shapes=[pltpu.VMEM((rows, d), BF16)],
        compiler_params=_cparams(("arbitrary", "arbitrary")),
        name="proj",
    )(x, g.reshape(1, d), w)


def _band_bias(slope, step):
    qi = lax.broadcasted_iota(jnp.int32, (BLK, 2 * BLK), 0)
    kj = lax.broadcasted_iota(jnp.int32, (BLK, 2 * BLK), 1)
    dist = qi + BLK - kj
    valid = (dist >= 0) & (dist <= N_KEYS)
    return jnp.where(valid, -slope * (dist * step).astype(F32) * LOG2E, NEG)


def _head_slope(h, n_heads):
    return jnp.exp2(jnp.full((1, 1), -8.0 / n_heads, F32) * (h + 1).astype(F32))


def _low_lanes(rows):
    return lax.broadcasted_iota(jnp.int32, (rows, LANES), 1) < HEAD_DIM


def _fill_ones(ones_s):
    ones_s[0] = jnp.where(_low_lanes(2 * BLK), 1.0, 0.0).astype(BF16)
    ones_s[1] = jnp.where(_low_lanes(2 * BLK), 0.0, 1.0).astype(BF16)


def _run_blocks(units, load, ones_s, finish, sinks_of=None):
    lo = _low_lanes(BLK)

    def scores(unit):
        q, kk, vv, bias = load(unit)
        zero = jnp.zeros_like(q)
        q2 = jnp.concatenate([jnp.where(lo, q, zero), jnp.where(lo, zero, q)], axis=0)
        return lax.dot_general(q2, kk, (((1,), (1,)), ((), ())), preferred_element_type=F32) + bias, vv

    def attend(s, vv, sinks):
        nk = vv.shape[0]
        halves = (s[:BLK], s[BLK:])
        m = [jnp.max(h, axis=1, keepdims=True) for h in halves]
        if sinks is not None:
            m = [jnp.maximum(mh, sk) for mh, sk in zip(m, sinks)]
        lhs = jnp.concatenate([jnp.exp2(h - mh).astype(BF16) for h, mh in zip(halves, m)], axis=1)
        zero = jnp.zeros_like(vv)
        rhs = jnp.concatenate(
            [jnp.concatenate([jnp.where(_low_lanes(nk), vv, zero), ones_s[0, :nk, :]], axis=1),
             jnp.concatenate([jnp.where(_low_lanes(nk), zero, vv), ones_s[1, :nk, :]], axis=1)], axis=0)
        ol = jnp.dot(lhs, rhs, preferred_element_type=F32)
        mt = jnp.where(lo, jnp.broadcast_to(m[0], (BLK, LANES)), jnp.broadcast_to(m[1], (BLK, LANES)))
        den = ol[:, LANES:]
        if sinks is not None:
            den = den + jnp.exp2(jnp.where(lo, sinks[0], sinks[1]) - mt)
        return ol[:, :LANES], den, mt

    results = []
    nxt = scores(units[0])
    for k, unit in enumerate(units):
        s, vv = nxt
        if k + 1 < len(units):
            nxt = scores(units[k + 1])
        results.append(attend(s, vv, None if sinks_of is None else sinks_of(unit)))
    for unit, res in zip(units, results):
        finish(unit, res)


def _attn_a_kernel(q_ref, k_ref, v_ref, g_ref, y_ref, kt_ref, vt_ref,
                   q1_s, k1_s, v1_s, q4_s, k4_s, v4_s, q16_s, k16_s, v16_s,
                   by4_s, st16_s, st4_s, bias_s, ones_s, *, seq):
    keep = kt_ref.shape[-1]
    for src, dst in ((k_ref, kt_ref), (v_ref, vt_ref)):
        for j in range(keep // LANES):
            tile = src[seq - keep + j * LANES:seq - keep + (j + 1) * LANES, :]
            dst[:, :, j * LANES:(j + 1) * LANES] = tile.T.reshape(2, HEAD_DIM, LANES)

    hp = pl.program_id(1)
    steps = tuple(d for _, d in DILATIONS)

    for pi, step in enumerate(steps):
        for hh in range(2):
            bias_s[pi, hh * BLK:(hh + 1) * BLK, :] = _band_bias(_head_slope(2 * hp + hh, A_HEADS), step)
    _fill_ones(ones_s)

    assert steps == (1, 4, 16)
    for src, dsts in ((q_ref, (q1_s, q4_s, q16_s)), (k_ref, (k1_s, k4_s, k16_s)), (v_ref, (v1_s, v4_s, v16_s))):
        dsts[0][0] = src[...].astype(BF16)
        for r in range(4):
            rows = src[pl.ds(r, seq // 4, stride=4), :]
            by4_s[r] = rows
            dsts[1][r] = rows.astype(BF16)
        for r in range(16):
            dsts[2][r] = by4_s[r % 4, pl.ds(r // 4, seq // 16, stride=4), :].astype(BF16)

    def merge(a, b):
        m = jnp.maximum(a[2], b[2])
        wa = jnp.exp2(a[2] - m)
        wb = jnp.exp2(b[2] - m)
        return wa * a[0] + wb * b[0], wa * a[1] + wb * b[1], m

    def blocks(pi, refs, n_res, finish):
        nb = seq // steps[pi] // BLK
        q_s, k_s, v_s = refs

        def load(unit):
            r, row0, first = unit
            if first:
                keys, bias = pl.ds(row0, BLK), bias_s[pi, :, BLK:]
            else:
                keys, bias = pl.ds(row0 - BLK, 2 * BLK), bias_s[pi]
            return q_s[r, pl.ds(row0, BLK), :], k_s[r, keys, :], v_s[r, keys, :], bias

        def run(units):
            _run_blocks(units, load, ones_s, finish)

        def trips(n, unit_of):
            if n == 0:
                return
            per = max(u for u in range(1, A_UNITS + 1) if n % u == 0)
            if n == per:
                run([unit_of(i) for i in range(n)])
                return

            def trip(i, c):
                run([unit_of(i * per + u) for u in range(per)])
                return c
            lax.fori_loop(0, n // per, trip, 0)

        row = lambda b: b * BLK if isinstance(b, int) else pl.multiple_of(b * BLK, BLK)
        if n_res > 1:
            trips(n_res, lambda idx: (idx, 0, True))
            trips(n_res * (nb - 1), lambda idx: (idx // (nb - 1), row(idx % (nb - 1) + 1), False))
        else:
            run([(0, 0, True)] + [(0, row(b), False) for b in range(1, A_UNITS)])
            trips(nb - A_UNITS, lambda idx: (0, row(idx + A_UNITS), False))

    def finish16(unit, res):
        r, row0, _ = unit
        rows = pl.ds(r // 4 + 4 * row0, BLK, stride=4)
        for c in range(3):
            st16_s[c, r % 4, rows, :] = res[c]
    blocks(2, (q16_s, k16_s, v16_s), 16, finish16)

    def finish4(unit, res):
        r, row0, _ = unit
        res = merge(tuple(st16_s[c, r, pl.ds(row0, BLK), :] for c in range(3)), res)
        rows = pl.ds(r + 4 * row0, BLK, stride=4)
        for c in range(3):
            st4_s[c, rows, :] = res[c]
    blocks(1, (q4_s, k4_s, v4_s), 4, finish4)

    def finish1(unit, res):
        _, row0, _ = unit
        rows = pl.ds(row0, BLK)
        o, l, _ = merge(tuple(st4_s[c, rows, :] for c in range(3)), res)
        y_ref[rows, :] = (o / l * _silu(g_ref[rows, :])).astype(BF16)
    blocks(0, (q1_s, k1_s, v1_s), 1, finish1)


def _attn_a(z, batch, seq):
    assert DILATIONS == ((N_KEYS, 1), (4 * N_KEYS, 4), (16 * N_KEYS, 16)) and N_KEYS == BLK
    assert seq % (16 * BLK) == 0 and seq // 16 >= 2 * BLK and seq // BLK >= A_UNITS
    keep = min(A_REACH, seq)
    spec = lambda off: pl.BlockSpec((None, seq, LANES), lambda b, hp: (off // LANES + hp, b, 0))
    res = lambda d, dt: pltpu.VMEM((d, seq // d, LANES), dt)
    kv_spec = pl.BlockSpec((None, 2, HEAD_DIM, keep), lambda b, hp: (b, hp, 0, 0))
    kv_shape = jax.ShapeDtypeStruct((batch, A_HEADS, HEAD_DIM, keep), F32)
    return pl.pallas_call(
        functools.partial(_attn_a_kernel, seq=seq),
        grid=(batch, A_CB),
        in_specs=[spec(QA0), spec(KA0), spec(VA0), spec(GA0)],
        out_specs=[pl.BlockSpec((None, seq, LANES), lambda b, hp: (hp, b, 0)), kv_spec, kv_spec],
        out_shape=[jax.ShapeDtypeStruct((A_CB, batch * seq, LANES), BF16), kv_shape, kv_shape],
        scratch_shapes=[res(d, BF16) for d in (1, 1, 1, 4, 4, 4, 16, 16, 16)]
                       + [res(4, F32), pltpu.VMEM((3, 4, seq // 4, LANES), F32), pltpu.VMEM((3, seq, LANES), F32),
                          pltpu.VMEM((3, 2 * BLK, 2 * BLK), F32), pltpu.VMEM((2, 2 * BLK, LANES), BF16)],
        compiler_params=_cparams(("parallel", "parallel")),
        name="attn_a",
    )(z, z, z, z)


def _attn_b_kernel(sink_ref, q_ref, kc_ref, kp_ref, vc_ref, vp_ref, g_ref, y_ref, kt_ref, vt_ref,
                   kk_s, vv_s, bias_s, ones_s):
    kvc = pl.program_id(1)
    ch = pl.program_id(2)
    heads = 2 * Q_PER_KV
    q_blocks = heads // 2

    @pl.when(ch == pl.num_programs(2) - 1)
    def _():
        keep = kt_ref.shape[-1]
        for src, dst in ((kc_ref, kt_ref), (vc_ref, vt_ref)):
            dst[...] = src[B_CHUNK - keep:, :].T.reshape(2, HEAD_DIM, keep)

    for cur, prev, dst in ((kc_ref, kp_ref, kk_s), (vc_ref, vp_ref, vv_s)):
        for src, a, n in ((prev, 0, BLK), (cur, BLK, B_CHUNK)):
            x = src[...]
            swapped = pltpu.roll(x, HEAD_DIM, 1)
            dst[0, a:a + n, :] = jnp.where(_low_lanes(n), x, swapped).astype(BF16)
            dst[1, a:a + n, :] = jnp.where(_low_lanes(n), swapped, x).astype(BF16)
    _fill_ones(ones_s)

    in_prev = lax.broadcasted_iota(jnp.int32, (BLK, 2 * BLK), 1) < BLK
    for hl in range(heads):
        full = _band_bias(_head_slope(kvc * heads + hl, B_HEADS), 1)
        rows = slice(hl % 2 * BLK, (hl % 2 + 1) * BLK)
        bias_s[0, hl // 2, rows, :] = full
        bias_s[1, hl // 2, rows, :] = jnp.where(in_prev, NEG, full)
    at_start = (ch == 0).astype(jnp.int32)

    def load(unit):
        i, qc = unit
        g = 2 * qc // Q_PER_KV
        keys = pl.ds(i * BLK, 2 * BLK)
        return (q_ref[qc, pl.ds(i * BLK, BLK), :].astype(BF16), kk_s[g, keys, :], vv_s[g, keys, :],
                bias_s[at_start if i == 0 else 0, qc])

    def sinks_of(unit):
        return tuple(sink_ref[kvc * heads + 2 * unit[1] + hh] * LOG2E for hh in range(2))

    def finish(unit, res):
        i, qc = unit
        rows = pl.ds(i * BLK, BLK)
        y_ref[qc, rows, :] = (res[0] / res[1] * _silu(g_ref[qc, rows, :])).astype(BF16)

    units = [(i, qc) for i in range(B_CHUNK // BLK) for qc in range(q_blocks)]
    for a in range(0, len(units), B_UNITS):
        _run_blocks(units[a:a + B_UNITS], load, ones_s, finish, sinks_of)


N_B_IN, N_B_OUT, N_B_SCRATCH = 7, 3, 4
N_DEC_IN, N_DEC_OUT = 10, 2


def _attn_b_decode_kernel(*refs, n_cast):
    ins, rest = refs[:N_B_IN + N_DEC_IN + n_cast], refs[N_B_IN + N_DEC_IN + n_cast:]
    b_out, dec_out = rest[:N_B_OUT], rest[N_B_OUT:N_B_OUT + N_DEC_OUT]
    cast_out = rest[N_B_OUT + N_DEC_OUT:N_B_OUT + N_DEC_OUT + n_cast]
    for src, dst in zip(ins[N_B_IN + N_DEC_IN:], cast_out):
        dst[...] = src[...].astype(BF16)
    _attn_b_kernel(*ins[:N_B_IN], *b_out, *rest[-N_B_SCRATCH:])
    _decode_kernel(*ins[N_B_IN:N_B_IN + N_DEC_IN], *dec_out)


def _attn_b(z, sinks, batch, seq, riders=None, first=0, to_bf16=()):
    assert seq % B_CHUNK == 0 and B_WINDOW == BLK
    nch = seq // B_CHUNK
    keep = min(B_WINDOW, seq)
    kv_spec = pl.BlockSpec((None, 2, HEAD_DIM, keep), lambda b, kvc, c: (b, kvc, 0, 0))
    kv_shape = jax.ShapeDtypeStruct((batch, B_KV_HEADS, HEAD_DIM, keep), F32)
    qpk = B_CB // KV_CB
    row = lambda b, c: b * nch + c
    prev = lambda b, c: jnp.maximum((b * nch + c) * (B_CHUNK // BLK) - 1, 0)
    cur_spec = lambda off: pl.BlockSpec((None, B_CHUNK, LANES),
                                        lambda b, kvc, c: (off // LANES + kvc, row(b, c), 0))
    prev_spec = lambda off: pl.BlockSpec((None, BLK, LANES),
                                         lambda b, kvc, c: (off // LANES + kvc, prev(b, c), 0))
    wide = lambda off: pl.BlockSpec((qpk, B_CHUNK, LANES),
                                    lambda b, kvc, c: (off // LANES // qpk + kvc, row(b, c), 0))
    in_specs = [pl.BlockSpec(memory_space=pltpu.SMEM),
                wide(QB0), cur_spec(KB0), prev_spec(KB0), cur_spec(VB0), prev_spec(VB0), wide(GB0)]
    out_specs = [pl.BlockSpec((qpk, B_CHUNK, LANES), lambda b, kvc, c: (kvc, row(b, c), 0)), kv_spec, kv_spec]
    out_shape = [jax.ShapeDtypeStruct((B_CB, batch * seq, LANES), BF16), kv_shape, kv_shape]
    operands = (sinks, z, z, z, z, z, z)
    assert len(in_specs) == N_B_IN and len(out_specs) == N_B_OUT
    body = _attn_b_kernel
    if riders is not None:
        groups, n_h = riders[7].shape[0], riders[7].shape[2]
        steps = batch * KV_CB * nch
        step = lambda b, kvc, c: (b * KV_CB + kvc) * nch + c
        assert steps % groups == 0 and first + steps // groups <= riders[0].shape[0]
        rider_specs, rider_shape = _decode_specs(
            riders, lambda b, kvc, c: (first + step(b, kvc, c) // groups, step(b, kvc, c) % groups))
        y_spec = pl.BlockSpec((None, None, HEAD_DIM, n_h),
                              lambda b, kvc, c: (step(b, kvc, c) // groups, step(b, kvc, c) % groups, 0, 0))
        assert all(w.shape[0] % (steps * 2 * SUBLANES) == 0 for w in to_bf16)
        cast_specs = [pl.BlockSpec((w.shape[0] // steps, w.shape[1]), lambda b, kvc, c: (step(b, kvc, c), 0))
                      for w in to_bf16]
        in_specs = in_specs + rider_specs + cast_specs
        out_specs = out_specs + [y_spec, y_spec] + cast_specs
        out_shape = out_shape + rider_shape(steps // groups) + [jax.ShapeDtypeStruct(w.shape, BF16) for w in to_bf16]
        operands = operands + tuple(riders) + tuple(to_bf16)
        assert len(rider_specs) == N_DEC_IN
        body = functools.partial(_attn_b_decode_kernel, n_cast=len(to_bf16))
    return pl.pallas_call(
        body,
        grid=(batch, KV_CB, nch),
        in_specs=in_specs,
        out_specs=out_specs,
        out_shape=out_shape,
        scratch_shapes=[pltpu.VMEM((2, BLK + B_CHUNK, LANES), BF16),
                        pltpu.VMEM((2, BLK + B_CHUNK, LANES), BF16),
                        pltpu.VMEM((2, Q_PER_KV, 2 * BLK, 2 * BLK), F32),
                        pltpu.VMEM((2, 2 * BLK, LANES), BF16)],
        compiler_params=_cparams(("arbitrary", "arbitrary", "arbitrary")),
        name="attn_b",
    )(*operands)


def _decode_kernel(zt_ref, ka_ref, va_ref, kb_ref, vb_ref, bias_a_ref, bias_b_ref, sink_ref, sel_ref, ones_ref,
                   ya_ref, yb_ref):
    n_h, n_kv = ya_ref.shape[-1], kb_ref.shape[0]
    zt = zt_ref[...]
    split_at = dict(zip(("qa", "ka", "va", "ga", "qb", "kb", "vb", "gb"),
                        (0, n_h, 2 * n_h, 3 * n_h, 4 * n_h, 5 * n_h, 5 * n_h + n_kv, 5 * n_h + 2 * n_kv)))
    cols = lambda name: zt[:, split_at[name]:split_at[name] + n_h]
    eye = lax.broadcasted_iota(jnp.int32, (n_h, n_h), 0) == lax.broadcasted_iota(jnp.int32, (n_h, n_h), 1)
    to_col = lambda row: jnp.sum(jnp.where(eye, jnp.broadcast_to(row, (n_h, n_h)), 0.0), axis=1, keepdims=True)
    to_row = lambda c: jnp.sum(jnp.where(eye, jnp.broadcast_to(c, (n_h, n_h)), 0.0), axis=0, keepdims=True)

    def mixer(q, k_ref, v_ref, kv_head, bias, k_new, v_new, self_bias, sink):
        t = k_ref.shape[-1]
        parts = [jnp.sum((k_ref[kv_head(h)] * q[:, h:h + 1]).reshape(HEAD_DIM // SUBLANES, SUBLANES, t), axis=0)
                 for h in range(n_h)]
        s = jnp.dot(sel_ref[...], jnp.concatenate(parts, axis=0).astype(BF16),
                    preferred_element_type=F32) + bias
        s_self = jnp.sum(q * k_new, axis=0, keepdims=True) + self_bias
        m = jnp.maximum(to_row(jnp.max(s, axis=1, keepdims=True)), s_self)
        if sink is not None:
            m = jnp.maximum(m, sink)
        p = jnp.exp2(s - to_col(m))
        p_self = jnp.exp2(s_self - m)
        den = to_row(jnp.sum(p, axis=1, keepdims=True)) + p_self
        if sink is not None:
            den = den + jnp.exp2(sink - m)
        sums = []
        for h in range(n_h):
            w = v_ref[kv_head(h)] * p[h:h + 1, :]
            acc = w[:, :LANES]
            for j in range(1, t // LANES):
                acc = acc + w[:, j * LANES:(j + 1) * LANES]
            sums.append(acc)
        o = jnp.dot(jnp.concatenate(sums, axis=1).astype(BF16), ones_ref[...], preferred_element_type=F32)
        return (o + v_new * p_self) / den

    oa = mixer(cols("qa"), ka_ref, va_ref, lambda h: h, bias_a_ref[...], cols("ka"), cols("va"),
               math.log2(len(DILATIONS)), None)
    ya_ref[...] = (oa * _silu(cols("ga"))).astype(BF16)

    kv_of_lane = lax.broadcasted_iota(jnp.int32, (HEAD_DIM, n_h), 1) // Q_PER_KV
    per_query = lambda name: sum(jnp.where(kv_of_lane == g, zt[:, split_at[name] + g:split_at[name] + g + 1], 0.0)
                                 for g in range(n_kv))
    ob = mixer(cols("qb"), kb_ref, vb_ref, lambda h: h // Q_PER_KV, bias_b_ref[...], per_query("kb"),
               per_query("vb"), 0.0, sink_ref[...])
    yb_ref[...] = (ob * _silu(cols("gb"))).astype(BF16)


def _decode_operands(z_s, cache_a_k, cache_a_v, cache_b_k, cache_b_v, sinks, groups):
    n = z_s.shape[0]
    assert cache_a_k.shape[1] == A_REACH and cache_b_k.shape[1] == B_WINDOW == N_KEYS
    n_h, n_kv = A_HEADS // groups, B_KV_HEADS // groups
    assert A_HEADS == B_HEADS and n_h * groups == A_HEADS and n_kv * groups == B_KV_HEADS
    slopes = lambda nh: jnp.exp2(-8.0 * jnp.arange(1, nh + 1, dtype=F32) / nh)
    dd = A_REACH - jnp.arange(A_REACH)
    mult = sum(((dd % d == 0) & (dd <= w)).astype(F32) for w, d in DILATIONS)
    bias_a = jnp.where(mult > 0, -slopes(A_HEADS)[:, None] * dd.astype(F32) * LOG2E
                       + jnp.log2(jnp.maximum(mult, 1.0)), NEG)
    bias_b = -slopes(B_HEADS)[:, None] * (B_WINDOW - jnp.arange(B_WINDOW)).astype(F32) * LOG2E
    by_head = lambda c: c.transpose(0, 2, 3, 1)
    sink2 = (sinks * LOG2E).reshape(groups, 1, n_h)
    heads = z_s.reshape(n, D_IN // HEAD_DIM, HEAD_DIM)
    share = lambda g: jnp.concatenate(
        [heads[:, off // HEAD_DIM + g * w:off // HEAD_DIM + (g + 1) * w]
         for off, w in zip((QA0, KA0, VA0, GA0, QB0, KB0, VB0, GB0), (n_h,) * 5 + (n_kv,) * 2 + (n_h,))], axis=1)
    zt = jnp.stack([share(g) for g in range(groups)], axis=1).transpose(0, 1, 3, 2)
    sel = (jnp.arange(n_h * SUBLANES)[None, :] // SUBLANES == jnp.arange(n_h)[:, None]).astype(BF16)
    ones = (jnp.arange(n_h * LANES)[:, None] // LANES == jnp.arange(n_h)[None, :]).astype(BF16)
    return (zt, by_head(cache_a_k), by_head(cache_a_v), by_head(cache_b_k), by_head(cache_b_v),
            bias_a, bias_b, sink2, sel, ones)


def _decode_specs(operands, where):
    zt, ka, va, kb, vb, bias_a, bias_b, sink2, sel, ones = operands
    groups, n_h, n_kv = sink2.shape[0], sink2.shape[2], kb.shape[1] // sink2.shape[0]
    at = lambda pick: (lambda *g: pick(*where(*g)))
    buf_spec = lambda a, heads: pl.BlockSpec((None, heads) + a.shape[2:], at(lambda s, grp: (s, grp, 0, 0)))
    rows_spec = lambda a: pl.BlockSpec((n_h, a.shape[1]), at(lambda s, grp: (grp, 0)))
    const = lambda a: pl.BlockSpec(a.shape, lambda *g: (0,) * a.ndim)
    in_specs = [pl.BlockSpec((None, None) + zt.shape[2:], at(lambda s, grp: (s, grp, 0, 0))),
                buf_spec(ka, n_h), buf_spec(va, n_h), buf_spec(kb, n_kv), buf_spec(vb, n_kv),
                rows_spec(bias_a), rows_spec(bias_b),
                pl.BlockSpec((None, 1, n_h), at(lambda s, grp: (grp, 0, 0))), const(sel), const(ones)]
    out_shape = lambda n: [jax.ShapeDtypeStruct((n, groups, HEAD_DIM, n_h), BF16)] * 2
    return in_specs, out_shape


def _decode(operands, first, count):
    groups = operands[7].shape[0]
    n_h = operands[7].shape[2]
    in_specs, out_shape = _decode_specs(operands, lambda i: (first + i // groups, i % groups))
    out_spec = pl.BlockSpec((None, None, HEAD_DIM, n_h), lambda i: (i // groups, i % groups, 0, 0))
    return pl.pallas_call(
        _decode_kernel,
        grid=(count * groups,),
        in_specs=in_specs,
        out_specs=[out_spec, out_spec],
        out_shape=out_shape(count),
        compiler_params=_cparams(("parallel",)),
        name="decode",
    )(*operands)


def _proj_decode_kernel(x_ref, g_ref, w_ref, *refs):
    *dec_in, z_ref, ya_ref, yb_ref, hn_ref = refs

    @pl.when(pl.program_id(1) == 0)
    def _():
        hn_ref[...] = _rms(x_ref[...], g_ref[...]).astype(BF16)

    _proj_tile(hn_ref[...], w_ref, z_ref)
    _decode_kernel(*dec_in, ya_ref, yb_ref)


def _proj_decode(x, g, w_bf, operands, tm):
    rows, d = x.shape
    n_col = D_IN // PROJ_TN
    steps = rows // tm * n_col
    groups, n_h = operands[7].shape[0], operands[7].shape[2]
    assert groups == 1 and steps <= operands[0].shape[0]
    in_specs, out_shape = _decode_specs(operands, lambda i, j: (i * n_col + j, 0))
    y_spec = pl.BlockSpec((None, None, HEAD_DIM, n_h), lambda i, j: (i * n_col + j, 0, 0, 0))
    return pl.pallas_call(
        _proj_decode_kernel,
        grid=(rows // tm, n_col),
        in_specs=[pl.BlockSpec((tm, d), lambda i, j: (i, 0), pipeline_mode=pl.Buffered(1)),
                  pl.BlockSpec((1, d), lambda i, j: (0, 0)),
                  pl.BlockSpec((d, PROJ_TN), lambda i, j: (0, j))] + in_specs,
        out_specs=[pl.BlockSpec((PROJ_TN // LANES, tm, LANES), lambda i, j: (j, i, 0)), y_spec, y_spec],
        out_shape=[jax.ShapeDtypeStruct((N_CB, rows, LANES), F32)] + out_shape(steps),
        scratch_shapes=[pltpu.VMEM((tm, d), BF16)],
        compiler_params=_cparams(("parallel", "arbitrary")),
        name="proj_decode",
    )(x, g.reshape(1, d), w_bf, *operands)


def _out_kernel(x_ref, ya_ref, yb_ref, p_ref, wout_ref, gple_ref, wpg_ref, wple_ref, gfin_ref, o_ref):
    y = jnp.concatenate([ya_ref[c] for c in range(A_CB)] + [yb_ref[c] for c in range(B_CB)], axis=1)
    h = x_ref[...] + jnp.dot(y, wout_ref[...], preferred_element_type=F32)
    gate = jax.nn.sigmoid(jnp.dot(_rms(h, gple_ref[...]).astype(BF16), wpg_ref[...],
                                  preferred_element_type=F32))
    ple = jnp.dot(p_ref[...].astype(BF16), wple_ref[...], preferred_element_type=F32)
    o_ref[...] = _rms(h + ple * gate, gfin_ref[...])


N_OUT_REFS = 10


def _out_decode_kernel(*refs):
    _out_kernel(*refs[:N_OUT_REFS - 1], refs[-3])
    _decode_kernel(*refs[N_OUT_REFS - 1:-3], refs[-2], refs[-1])


def _out(x, ya, yb, p, wout_bf, g_ple, wpg_bf, wple_bf, g_final, tm, riders=None, first=0):
    rows, d = x.shape
    resident = lambda a: pl.BlockSpec(a.shape, lambda i: (0,) * a.ndim, pipeline_mode=pl.Buffered(1))
    g_ple, g_final = g_ple.reshape(1, d), g_final.reshape(1, d)
    in_specs = [pl.BlockSpec((tm, d), lambda i: (i, 0)),
                pl.BlockSpec((A_CB, tm, LANES), lambda i: (0, i, 0)),
                pl.BlockSpec((B_CB, tm, LANES), lambda i: (0, i, 0)),
                pl.BlockSpec((tm, p.shape[1]), lambda i: (i, 0)),
                resident(wout_bf), resident(g_ple), resident(wpg_bf), resident(wple_bf), resident(g_final)]
    out_specs = [pl.BlockSpec((tm, d), lambda i: (i, 0))]
    out_shape = [jax.ShapeDtypeStruct((rows, d), F32)]
    operands = (x, ya, yb, p, wout_bf, g_ple, wpg_bf, wple_bf, g_final)
    assert len(operands) + 1 == N_OUT_REFS
    if riders is None:
        return pl.pallas_call(_out_kernel, grid=(rows // tm,), in_specs=in_specs, out_specs=out_specs[0],
                              out_shape=out_shape[0], compiler_params=_cparams(("parallel",)), name="out")(*operands)
    groups, n_h = riders[7].shape[0], riders[7].shape[2]
    steps = rows // tm
    assert steps % groups == 0 and first + steps // groups <= riders[0].shape[0]
    rider_specs, rider_shape = _decode_specs(riders, lambda i: (first + i // groups, i % groups))
    y_spec = pl.BlockSpec((None, None, HEAD_DIM, n_h), lambda i: (i // groups, i % groups, 0, 0))
    return pl.pallas_call(
        _out_decode_kernel,
        grid=(steps,),
        in_specs=in_specs + rider_specs,
        out_specs=out_specs + [y_spec, y_spec],
        out_shape=out_shape + rider_shape(steps // groups),
        compiler_params=_cparams(("arbitrary",)),
        name="out_decode",
    )(*operands, *riders)


def _col_blocks(y):
    return y.reshape(y.shape[0], y.shape[1] // LANES, LANES).transpose(1, 0, 2)


def kernel(x_prompt, x_sample, cache_a_k, cache_a_v, cache_b_k, cache_b_v, p_prompt, p_sample,
           g_mix, w_in, sinks, w_out, g_ple, w_pg, w_ple, g_final):
    depth = w_in.shape[0]
    batch, seq, d = x_prompt.shape
    n_s, t_s, _ = x_sample.shape
    assert depth == 1 and t_s == 1 and w_in.shape[2] == D_IN
    xp = x_prompt.reshape(batch * seq, d)
    xs = x_sample.reshape(n_s, d)
    i = 0
    z_s, w_in_bf = _proj(xs, g_mix[i], w_in[i])
    z_s = z_s.transpose(1, 0, 2).reshape(n_s, D_IN)
    dec_whole, dec_halves = (_decode_operands(z_s, cache_a_k[i], cache_a_v[i], cache_b_k[i], cache_b_v[i],
                                              sinks[i], groups) for groups in (1, 2))

    z, *y_s0 = _proj_decode(xp, g_mix[i], w_in_bf, dec_whole, tm=1024)
    ya, ak_t, av_t = _attn_a(z, batch, seq)
    n0 = y_s0[0].shape[0]
    yb, bk_t, bv_t, *rode_b = _attn_b(z, sinks[i], batch, seq, riders=dec_halves, first=n0,
                                      to_bf16=(w_out[i], w_pg[i], w_ple[i]))
    y_s1, (w_out_bf, w_pg_bf, w_ple_bf) = rode_b[:N_DEC_OUT], rode_b[N_DEC_OUT:]
    n1 = n0 + y_s1[0].shape[0]
    y_prompt, *y_s2 = _out(xp, ya, yb, p_prompt[i].reshape(batch * seq, -1), w_out_bf, g_ple[i], w_pg_bf, w_ple_bf,
                           g_final, tm=256, riders=dec_halves, first=n1)
    new_p = [t.transpose(0, 3, 1, 2)[None] for t in (ak_t, av_t, bk_t, bv_t)]

    n2 = n1 + y_s2[0].shape[0]
    rode = [y_s0, y_s1, y_s2] + ([_decode(dec_whole, n2, n_s - n2)] if n2 < n_s else [])
    rows_of = lambda t: t.transpose(0, 1, 3, 2).reshape(t.shape[0], A_WIDTH)
    ya_s, yb_s = (jnp.concatenate([rows_of(t) for t in parts]) for parts in zip(*rode))
    y_sample = _out(xs, _col_blocks(ya_s), _col_blocks(yb_s), p_sample[i].reshape(n_s, -1), w_out_bf, g_ple[i],
                    w_pg_bf, w_ple_bf, g_final, tm=n_s)
    new_s = [z_s[:, off:off + width].reshape(1, n_s, 1, width // HEAD_DIM, HEAD_DIM)
             for off, width in ((KA0, A_WIDTH), (VA0, A_WIDTH), (KB0, B_KV_WIDTH), (VB0, B_KV_WIDTH))]

    return (y_prompt.reshape(batch, seq, d), y_sample.reshape(n_s, t_s, d), *new_p, *new_s)
```

```python
import functools
import math

import jax
import jax.numpy as jnp
from jax import lax
from jax.experimental import pallas as pl
from jax.experimental.pallas import tpu as pltpu

F32 = jnp.float32
BF16 = jnp.bfloat16

LANES = 128
SUBLANES = 8
HEAD_DIM = 64
A_HEADS = 16
B_HEADS = 16
B_KV_HEADS = 4
BLK = 128
N_KEYS = 128
DILATIONS = ((128, 1), (512, 4), (2048, 16))
A_REACH = 2048
B_WINDOW = 128
EPS = 1e-6
NEG = -1e30

A_WIDTH = A_HEADS * HEAD_DIM
B_WIDTH = B_HEADS * HEAD_DIM
B_KV_WIDTH = B_KV_HEADS * HEAD_DIM
IN_SPLITS = (A_WIDTH, A_WIDTH, A_WIDTH, A_WIDTH, B_WIDTH, B_KV_WIDTH, B_KV_WIDTH, B_WIDTH)
D_IN = sum(IN_SPLITS)
N_CB = D_IN // LANES
QA0, KA0, VA0, GA0, QB0, KB0, VB0, GB0 = (sum(IN_SPLITS[:i]) for i in range(8))
A_CB = A_WIDTH // LANES
B_CB = B_WIDTH // LANES
KV_CB = B_KV_WIDTH // LANES
Q_PER_KV = B_HEADS // B_KV_HEADS

PROJ_TN = 512
B_CHUNK = 1024
A_UNITS = 16
B_UNITS = 16
LOG2E = math.log2(math.e)
Q_SCALE = HEAD_DIM ** -0.5 * LOG2E
VMEM_LIMIT = 63 * 1024 * 1024


def _cparams(sem):
    return pltpu.CompilerParams(dimension_semantics=sem, vmem_limit_bytes=VMEM_LIMIT)


def _rms(x, g):
    return x * lax.rsqrt(jnp.mean(x * x, axis=-1, keepdims=True) + EPS) * g


def _silu(g):
    return g * jax.nn.sigmoid(g)


def _proj_tile(hn, w_ref, z_ref):
    acc = jnp.dot(hn, w_ref[...], preferred_element_type=F32)
    col = pl.program_id(1) * PROJ_TN
    is_q = ((col >= QA0) & (col < QA0 + A_WIDTH)) | ((col >= QB0) & (col < QB0 + B_WIDTH))
    acc = acc * jnp.where(is_q, Q_SCALE, 1.0)
    for c in range(PROJ_TN // LANES):
        z_ref[c] = acc[:, c * LANES:(c + 1) * LANES]


def _proj_kernel(x_ref, g_ref, w_ref, z_ref, wb_ref, hn_ref):
    @pl.when(pl.program_id(1) == 0)
    def _():
        hn_ref[...] = _rms(x_ref[...], g_ref[...]).astype(BF16)

    wb_ref[...] = w_ref[...].astype(BF16)
    _proj_tile(hn_ref[...], wb_ref, z_ref)


def _proj(x, g, w):
    rows, d = x.shape
    w_spec = pl.BlockSpec((d, PROJ_TN), lambda i, j: (0, j))
    return pl.pallas_call(
        _proj_kernel,
        grid=(1, D_IN // PROJ_TN),
        in_specs=[pl.BlockSpec((rows, d), lambda i, j: (0, 0)), pl.BlockSpec((1, d), lambda i, j: (0, 0)), w_spec],
        out_specs=[pl.BlockSpec((PROJ_TN // LANES, rows, LANES), lambda i, j: (j, 0, 0)), w_spec],
        out_shape=[jax.ShapeDtypeStruct((N_CB, rows, LANES), F32), jax.ShapeDtypeStruct(w.shape, BF16)],
        scratch_shapes=[pltpu.VMEM((rows, d), BF16)],
        compiler_params=_cparams(("arbitrary", "arbitrary")),
        name="proj",
    )(x, g.reshape(1, d), w)


def _band_bias(slope, step):
    qi = lax.broadcasted_iota(jnp.int32, (BLK, 2 * BLK), 0)
    kj = lax.broadcasted_iota(jnp.int32, (BLK, 2 * BLK), 1)
    dist = qi + BLK - kj
    valid = (dist >= 0) & (dist <= N_KEYS)
    return jnp.where(valid, -slope * (dist * step).astype(F32) * LOG2E, NEG)


def _head_slope(h, n_heads):
    return jnp.exp2(jnp.full((1, 1), -8.0 / n_heads, F32) * (h + 1).astype(F32))


def _low_lanes(rows):
    return lax.broadcasted_iota(jnp.int32, (rows, LANES), 1) < HEAD_DIM


def _fill_ones(ones_s):
    ones_s[0] = jnp.where(_low_lanes(2 * BLK), 1.0, 0.0).astype(BF16)
    ones_s[1] = jnp.where(_low_lanes(2 * BLK), 0.0, 1.0).astype(BF16)


def _run_blocks(units, load, ones_s, finish, sinks_of=None):
    lo = _low_lanes(BLK)

    def scores(unit):
        q, kk, vv, bias = load(unit)
        zero = jnp.zeros_like(q)
        q2 = jnp.concatenate([jnp.where(lo, q, zero), jnp.where(lo, zero, q)], axis=0)
        return lax.dot_general(q2, kk, (((1,), (1,)), ((), ())), preferred_element_type=F32) + bias, vv

    def attend(s, vv, sinks):
        nk = vv.shape[0]
        halves = (s[:BLK], s[BLK:])
        m = [jnp.max(h, axis=1, keepdims=True) for h in halves]
        if sinks is not None:
            m = [jnp.maximum(mh, sk) for mh, sk in zip(m, sinks)]
        lhs = jnp.concatenate([jnp.exp2(h - mh).astype(BF16) for h, mh in zip(halves, m)], axis=1)
        zero = jnp.zeros_like(vv)
        rhs = jnp.concatenate(
            [jnp.concatenate([jnp.where(_low_lanes(nk), vv, zero), ones_s[0, :nk, :]], axis=1),
             jnp.concatenate([jnp.where(_low_lanes(nk), zero, vv), ones_s[1, :nk, :]], axis=1)], axis=0)
        ol = jnp.dot(lhs, rhs, preferred_element_type=F32)
        mt = jnp.where(lo, jnp.broadcast_to(m[0], (BLK, LANES)), jnp.broadcast_to(m[1], (BLK, LANES)))
        den = ol[:, LANES:]
        if sinks is not None:
            den = den + jnp.exp2(jnp.where(lo, sinks[0], sinks[1]) - mt)
        return ol[:, :LANES], den, mt

    results = []
    nxt = scores(units[0])
    for k, unit in enumerate(units):
        s, vv = nxt
        if k + 1 < len(units):
            nxt = scores(units[k + 1])
        results.append(attend(s, vv, None if sinks_of is None else sinks_of(unit)))
    for unit, res in zip(units, results):
        finish(unit, res)


def _attn_a_kernel(q_ref, k_ref, v_ref, g_ref, y_ref, kt_ref, vt_ref,
                   q1_s, k1_s, v1_s, q4_s, k4_s, v4_s, q16_s, k16_s, v16_s,
                   by4_s, st16_s, st4_s, bias_s, ones_s, *, seq):
    keep = kt_ref.shape[-1]
    for src, dst in ((k_ref, kt_ref), (v_ref, vt_ref)):
        for j in range(keep // LANES):
            tile = src[seq - keep + j * LANES:seq - keep + (j + 1) * LANES, :]
            dst[:, :, j * LANES:(j + 1) * LANES] = tile.T.reshape(2, HEAD_DIM, LANES)

    hp = pl.program_id(1)
    steps = tuple(d for _, d in DILATIONS)

    for pi, step in enumerate(steps):
        for hh in range(2):
            bias_s[pi, hh * BLK:(hh + 1) * BLK, :] = _band_bias(_head_slope(2 * hp + hh, A_HEADS), step)
    _fill_ones(ones_s)

    assert steps == (1, 4, 16)
    for src, dsts in ((q_ref, (q1_s, q4_s, q16_s)), (k_ref, (k1_s, k4_s, k16_s)), (v_ref, (v1_s, v4_s, v16_s))):
        dsts[0][0] = src[...].astype(BF16)
        for r in range(4):
            rows = src[pl.ds(r, seq // 4, stride=4), :]
            by4_s[r] = rows
            dsts[1][r] = rows.astype(BF16)
        for r in range(16):
            dsts[2][r] = by4_s[r % 4, pl.ds(r // 4, seq // 16, stride=4), :].astype(BF16)

    def merge(a, b):
        m = jnp.maximum(a[2], b[2])
        wa = jnp.exp2(a[2] - m)
        wb = jnp.exp2(b[2] - m)
        return wa * a[0] + wb * b[0], wa * a[1] + wb * b[1], m

    def blocks(pi, refs, n_res, finish):
        nb = seq // steps[pi] // BLK
        q_s, k_s, v_s = refs

        def load(unit):
            r, row0, first = unit
            if first:
                keys, bias = pl.ds(row0, BLK), bias_s[pi, :, BLK:]
            else:
                keys, bias = pl.ds(row0 - BLK, 2 * BLK), bias_s[pi]
            return q_s[r, pl.ds(row0, BLK), :], k_s[r, keys, :], v_s[r, keys, :], bias

        def run(units):
            _run_blocks(units, load, ones_s, finish)

        def trips(n, unit_of):
            if n == 0:
                return
            per = max(u for u in range(1, A_UNITS + 1) if n % u == 0)
            if n == per:
                run([unit_of(i) for i in range(n)])
                return

            def trip(i, c):
                run([unit_of(i * per + u) for u in range(per)])
                return c
            lax.fori_loop(0, n // per, trip, 0)

        row = lambda b: b * BLK if isinstance(b, int) else pl.multiple_of(b * BLK, BLK)
        if n_res > 1:
            trips(n_res, lambda idx: (idx, 0, True))
            trips(n_res * (nb - 1), lambda idx: (idx // (nb - 1), row(idx % (nb - 1) + 1), False))
        else:
            run([(0, 0, True)] + [(0, row(b), False) for b in range(1, A_UNITS)])
            trips(nb - A_UNITS, lambda idx: (0, row(idx + A_UNITS), False))

    def finish16(unit, res):
        r, row0, _ = unit
        rows = pl.ds(r // 4 + 4 * row0, BLK, stride=4)
        for c in range(3):
            st16_s[c, r % 4, rows, :] = res[c]
    blocks(2, (q16_s, k16_s, v16_s), 16, finish16)

    def finish4(unit, res):
        r, row0, _ = unit
        res = merge(tuple(st16_s[c, r, pl.ds(row0, BLK), :] for c in range(3)), res)
        rows = pl.ds(r + 4 * row0, BLK, stride=4)
        for c in range(3):
            st4_s[c, rows, :] = res[c]
    blocks(1, (q4_s, k4_s, v4_s), 4, finish4)

    def finish1(unit, res):
        _, row0, _ = unit
        rows = pl.ds(row0, BLK)
        o, l, _ = merge(tuple(st4_s[c, rows, :] for c in range(3)), res)
        y_ref[rows, :] = (o / l * _silu(g_ref[rows, :])).astype(BF16)
    blocks(0, (q1_s, k1_s, v1_s), 1, finish1)


def _attn_a(z, batch, seq):
    assert DILATIONS == ((N_KEYS, 1), (4 * N_KEYS, 4), (16 * N_KEYS, 16)) and N_KEYS == BLK
    assert seq % (16 * BLK) == 0 and seq // 16 >= 2 * BLK and seq // BLK >= A_UNITS
    keep = min(A_REACH, seq)
    spec = lambda off: pl.BlockSpec((None, seq, LANES), lambda b, hp: (off // LANES + hp, b, 0))
    res = lambda d, dt: pltpu.VMEM((d, seq // d, LANES), dt)
    kv_spec = pl.BlockSpec((None, 2, HEAD_DIM, keep), lambda b, hp: (b, hp, 0, 0))
    kv_shape = jax.ShapeDtypeStruct((batch, A_HEADS, HEAD_DIM, keep), F32)
    return pl.pallas_call(
        functools.partial(_attn_a_kernel, seq=seq),
        grid=(batch, A_CB),
        in_specs=[spec(QA0), spec(KA0), spec(VA0), spec(GA0)],
        out_specs=[pl.BlockSpec((None, seq, LANES), lambda b, hp: (hp, b, 0)), kv_spec, kv_spec],
        out_shape=[jax.ShapeDtypeStruct((A_CB, batch * seq, LANES), BF16), kv_shape, kv_shape],
        scratch_shapes=[res(d, BF16) for d in (1, 1, 1, 4, 4, 4, 16, 16, 16)]
                       + [res(4, F32), pltpu.VMEM((3, 4, seq // 4, LANES), F32), pltpu.VMEM((3, seq, LANES), F32),
                          pltpu.VMEM((3, 2 * BLK, 2 * BLK), F32), pltpu.VMEM((2, 2 * BLK, LANES), BF16)],
        compiler_params=_cparams(("parallel", "parallel")),
        name="attn_a",
    )(z, z, z, z)


def _attn_b_kernel(sink_ref, q_ref, kc_ref, kp_ref, vc_ref, vp_ref, g_ref, y_ref, kt_ref, vt_ref,
                   kk_s, vv_s, bias_s, ones_s):
    kvc = pl.program_id(1)
    ch = pl.program_id(2)
    heads = 2 * Q_PER_KV
    q_blocks = heads // 2

    @pl.when(ch == pl.num_programs(2) - 1)
    def _():
        keep = kt_ref.shape[-1]
        for src, dst in ((kc_ref, kt_ref), (vc_ref, vt_ref)):
            dst[...] = src[B_CHUNK - keep:, :].T.reshape(2, HEAD_DIM, keep)

    for cur, prev, dst in ((kc_ref, kp_ref, kk_s), (vc_ref, vp_ref, vv_s)):
        for src, a, n in ((prev, 0, BLK), (cur, BLK, B_CHUNK)):
            x = src[...]
            swapped = pltpu.roll(x, HEAD_DIM, 1)
            dst[0, a:a + n, :] = jnp.where(_low_lanes(n), x, swapped).astype(BF16)
            dst[1, a:a + n, :] = jnp.where(_low_lanes(n), swapped, x).astype(BF16)
    _fill_ones(ones_s)

    in_prev = lax.broadcasted_iota(jnp.int32, (BLK, 2 * BLK), 1) < BLK
    for hl in range(heads):
        full = _band_bias(_head_slope(kvc * heads + hl, B_HEADS), 1)
        rows = slice(hl % 2 * BLK, (hl % 2 + 1) * BLK)
        bias_s[0, hl // 2, rows, :] = full
        bias_s[1, hl // 2, rows, :] = jnp.where(in_prev, NEG, full)
    at_start = (ch == 0).astype(jnp.int32)

    def load(unit):
        i, qc = unit
        g = 2 * qc // Q_PER_KV
        keys = pl.ds(i * BLK, 2 * BLK)
        return (q_ref[qc, pl.ds(i * BLK, BLK), :].astype(BF16), kk_s[g, keys, :], vv_s[g, keys, :],
                bias_s[at_start if i == 0 else 0, qc])

    def sinks_of(unit):
        return tuple(sink_ref[kvc * heads + 2 * unit[1] + hh] * LOG2E for hh in range(2))

    def finish(unit, res):
        i, qc = unit
        rows = pl.ds(i * BLK, BLK)
        y_ref[qc, rows, :] = (res[0] / res[1] * _silu(g_ref[qc, rows, :])).astype(BF16)

    units = [(i, qc) for i in range(B_CHUNK // BLK) for qc in range(q_blocks)]
    for a in range(0, len(units), B_UNITS):
        _run_blocks(units[a:a + B_UNITS], load, ones_s, finish, sinks_of)


N_B_IN, N_B_OUT, N_B_SCRATCH = 7, 3, 4
N_DEC_IN, N_DEC_OUT = 10, 2


def _attn_b_decode_kernel(*refs, n_cast):
    ins, rest = refs[:N_B_IN + N_DEC_IN + n_cast], refs[N_B_IN + N_DEC_IN + n_cast:]
    b_out, dec_out = rest[:N_B_OUT], rest[N_B_OUT:N_B_OUT + N_DEC_OUT]
    cast_out = rest[N_B_OUT + N_DEC_OUT:N_B_OUT + N_DEC_OUT + n_cast]
    for src, dst in zip(ins[N_B_IN + N_DEC_IN:], cast_out):
        dst[...] = src[...].astype(BF16)
    _attn_b_kernel(*ins[:N_B_IN], *b_out, *rest[-N_B_SCRATCH:])
    _decode_kernel(*ins[N_B_IN:N_B_IN + N_DEC_IN], *dec_out)


def _attn_b(z, sinks, batch, seq, riders=None, first=0, to_bf16=()):
    assert seq % B_CHUNK == 0 and B_WINDOW == BLK
    nch = seq // B_CHUNK
    keep = min(B_WINDOW, seq)
    kv_spec = pl.BlockSpec((None, 2, HEAD_DIM, keep), lambda b, kvc, c: (b, kvc, 0, 0))
    kv_shape = jax.ShapeDtypeStruct((batch, B_KV_HEADS, HEAD_DIM, keep), F32)
    qpk = B_CB // KV_CB
    row = lambda b, c: b * nch + c
    prev = lambda b, c: jnp.maximum((b * nch + c) * (B_CHUNK // BLK) - 1, 0)
    cur_spec = lambda off: pl.BlockSpec((None, B_CHUNK, LANES),
                                        lambda b, kvc, c: (off // LANES + kvc, row(b, c), 0))
    prev_spec = lambda off: pl.BlockSpec((None, BLK, LANES),
                                         lambda b, kvc, c: (off // LANES + kvc, prev(b, c), 0))
    wide = lambda off: pl.BlockSpec((qpk, B_CHUNK, LANES),
                                    lambda b, kvc, c: (off // LANES // qpk + kvc, row(b, c), 0))
    in_specs = [pl.BlockSpec(memory_space=pltpu.SMEM),
                wide(QB0), cur_spec(KB0), prev_spec(KB0), cur_spec(VB0), prev_spec(VB0), wide(GB0)]
    out_specs = [pl.BlockSpec((qpk, B_CHUNK, LANES), lambda b, kvc, c: (kvc, row(b, c), 0)), kv_spec, kv_spec]
    out_shape = [jax.ShapeDtypeStruct((B_CB, batch * seq, LANES), BF16), kv_shape, kv_shape]
    operands = (sinks, z, z, z, z, z, z)
    assert len(in_specs) == N_B_IN and len(out_specs) == N_B_OUT
    body = _attn_b_kernel
    if riders is not None:
        groups, n_h = riders[7].shape[0], riders[7].shape[2]
        steps = batch * KV_CB * nch
        step = lambda b, kvc, c: (b * KV_CB + kvc) * nch + c
        assert steps % groups == 0 and first + steps // groups <= riders[0].shape[0]
        rider_specs, rider_shape = _decode_specs(
            riders, lambda b, kvc, c: (first + step(b, kvc, c) // groups, step(b, kvc, c) % groups))
        y_spec = pl.BlockSpec((None, None, HEAD_DIM, n_h),
                              lambda b, kvc, c: (step(b, kvc, c) // groups, step(b, kvc, c) % groups, 0, 0))
        assert all(w.shape[0] % (steps * 2 * SUBLANES) == 0 for w in to_bf16)
        cast_specs = [pl.BlockSpec((w.shape[0] // steps, w.shape[1]), lambda b, kvc, c: (step(b, kvc, c), 0))
                      for w in to_bf16]
        in_specs = in_specs + rider_specs + cast_specs
        out_specs = out_specs + [y_spec, y_spec] + cast_specs
        out_shape = out_shape + rider_shape(steps // groups) + [jax.ShapeDtypeStruct(w.shape, BF16) for w in to_bf16]
        operands = operands + tuple(riders) + tuple(to_bf16)
        assert len(rider_specs) == N_DEC_IN
        body = functools.partial(_attn_b_decode_kernel, n_cast=len(to_bf16))
    return pl.pallas_call(
        body,
        grid=(batch, KV_CB, nch),
        in_specs=in_specs,
        out_specs=out_specs,
        out_shape=out_shape,
        scratch_shapes=[pltpu.VMEM((2, BLK + B_CHUNK, LANES), BF16),
                        pltpu.VMEM((2, BLK + B_CHUNK, LANES), BF16),
                        pltpu.VMEM((2, Q_PER_KV, 2 * BLK, 2 * BLK), F32),
                        pltpu.VMEM((2, 2 * BLK, LANES), BF16)],
        compiler_params=_cparams(("arbitrary", "arbitrary", "arbitrary")),
        name="attn_b",
    )(*operands)


def _decode_kernel(zt_ref, ka_ref, va_ref, kb_ref, vb_ref, bias_a_ref, bias_b_ref, sink_ref, sel_ref, ones_ref,
                   ya_ref, yb_ref):
    n_h, n_kv = ya_ref.shape[-1], kb_ref.shape[0]
    zt = zt_ref[...]
    split_at = dict(zip(("qa", "ka", "va", "ga", "qb", "kb", "vb", "gb"),
                        (0, n_h, 2 * n_h, 3 * n_h, 4 * n_h, 5 * n_h, 5 * n_h + n_kv, 5 * n_h + 2 * n_kv)))
    cols = lambda name: zt[:, split_at[name]:split_at[name] + n_h]
    eye = lax.broadcasted_iota(jnp.int32, (n_h, n_h), 0) == lax.broadcasted_iota(jnp.int32, (n_h, n_h), 1)
    to_col = lambda row: jnp.sum(jnp.where(eye, jnp.broadcast_to(row, (n_h, n_h)), 0.0), axis=1, keepdims=True)
    to_row = lambda c: jnp.sum(jnp.where(eye, jnp.broadcast_to(c, (n_h, n_h)), 0.0), axis=0, keepdims=True)

    def mixer(q, k_ref, v_ref, kv_head, bias, k_new, v_new, self_bias, sink):
        t = k_ref.shape[-1]
        parts = [jnp.sum((k_ref[kv_head(h)] * q[:, h:h + 1]).reshape(HEAD_DIM // SUBLANES, SUBLANES, t), axis=0)
                 for h in range(n_h)]
        s = jnp.dot(sel_ref[...], jnp.concatenate(parts, axis=0).astype(BF16),
                    preferred_element_type=F32) + bias
        s_self = jnp.sum(q * k_new, axis=0, keepdims=True) + self_bias
        m = jnp.maximum(to_row(jnp.max(s, axis=1, keepdims=True)), s_self)
        if sink is not None:
            m = jnp.maximum(m, sink)
        p = jnp.exp2(s - to_col(m))
        p_self = jnp.exp2(s_self - m)
        den = to_row(jnp.sum(p, axis=1, keepdims=True)) + p_self
        if sink is not None:
            den = den + jnp.exp2(sink - m)
        sums = []
        for h in range(n_h):
            w = v_ref[kv_head(h)] * p[h:h + 1, :]
            acc = w[:, :LANES]
            for j in range(1, t // LANES):
                acc = acc + w[:, j * LANES:(j + 1) * LANES]
            sums.append(acc)
        o = jnp.dot(jnp.concatenate(sums, axis=1).astype(BF16), ones_ref[...], preferred_element_type=F32)
        return (o + v_new * p_self) / den

    oa = mixer(cols("qa"), ka_ref, va_ref, lambda h: h, bias_a_ref[...], cols("ka"), cols("va"),
               math.log2(len(DILATIONS)), None)
    ya_ref[...] = (oa * _silu(cols("ga"))).astype(BF16)

    kv_of_lane = lax.broadcasted_iota(jnp.int32, (HEAD_DIM, n_h), 1) // Q_PER_KV
    per_query = lambda name: sum(jnp.where(kv_of_lane == g, zt[:, split_at[name] + g:split_at[name] + g + 1], 0.0)
                                 for g in range(n_kv))
    ob = mixer(cols("qb"), kb_ref, vb_ref, lambda h: h // Q_PER_KV, bias_b_ref[...], per_query("kb"),
               per_query("vb"), 0.0, sink_ref[...])
    yb_ref[...] = (ob * _silu(cols("gb"))).astype(BF16)


def _decode_operands(z_s, cache_a_k, cache_a_v, cache_b_k, cache_b_v, sinks, groups):
    n = z_s.shape[0]
    assert cache_a_k.shape[1] == A_REACH and cache_b_k.shape[1] == B_WINDOW == N_KEYS
    n_h, n_kv = A_HEADS // groups, B_KV_HEADS // groups
    assert A_HEADS == B_HEADS and n_h * groups == A_HEADS and n_kv * groups == B_KV_HEADS
    slopes = lambda nh: jnp.exp2(-8.0 * jnp.arange(1, nh + 1, dtype=F32) / nh)
    dd = A_REACH - jnp.arange(A_REACH)
    mult = sum(((dd % d == 0) & (dd <= w)).astype(F32) for w, d in DILATIONS)
    bias_a = jnp.where(mult > 0, -slopes(A_HEADS)[:, None] * dd.astype(F32) * LOG2E
                       + jnp.log2(jnp.maximum(mult, 1.0)), NEG)
    bias_b = -slopes(B_HEADS)[:, None] * (B_WINDOW - jnp.arange(B_WINDOW)).astype(F32) * LOG2E
    by_head = lambda c: c.transpose(0, 2, 3, 1)
    sink2 = (sinks * LOG2E).reshape(groups, 1, n_h)
    heads = z_s.reshape(n, D_IN // HEAD_DIM, HEAD_DIM)
    share = lambda g: jnp.concatenate(
        [heads[:, off // HEAD_DIM + g * w:off // HEAD_DIM + (g + 1) * w]
         for off, w in zip((QA0, KA0, VA0, GA0, QB0, KB0, VB0, GB0), (n_h,) * 5 + (n_kv,) * 2 + (n_h,))], axis=1)
    zt = jnp.stack([share(g) for g in range(groups)], axis=1).transpose(0, 1, 3, 2)
    sel = (jnp.arange(n_h * SUBLANES)[None, :] // SUBLANES == jnp.arange(n_h)[:, None]).astype(BF16)
    ones = (jnp.arange(n_h * LANES)[:, None] // LANES == jnp.arange(n_h)[None, :]).astype(BF16)
    return (zt, by_head(cache_a_k), by_head(cache_a_v), by_head(cache_b_k), by_head(cache_b_v),
            bias_a, bias_b, sink2, sel, ones)


def _decode_specs(operands, where):
    zt, ka, va, kb, vb, bias_a, bias_b, sink2, sel, ones = operands
    groups, n_h, n_kv = sink2.shape[0], sink2.shape[2], kb.shape[1] // sink2.shape[0]
    at = lambda pick: (lambda *g: pick(*where(*g)))
    buf_spec = lambda a, heads: pl.BlockSpec((None, heads) + a.shape[2:], at(lambda s, grp: (s, grp, 0, 0)))
    rows_spec = lambda a: pl.BlockSpec((n_h, a.shape[1]), at(lambda s, grp: (grp, 0)))
    const = lambda a: pl.BlockSpec(a.shape, lambda *g: (0,) * a.ndim)
    in_specs = [pl.BlockSpec((None, None) + zt.shape[2:], at(lambda s, grp: (s, grp, 0, 0))),
                buf_spec(ka, n_h), buf_spec(va, n_h), buf_spec(kb, n_kv), buf_spec(vb, n_kv),
                rows_spec(bias_a), rows_spec(bias_b),
                pl.BlockSpec((None, 1, n_h), at(lambda s, grp: (grp, 0, 0))), const(sel), const(ones)]
    out_shape = lambda n: [jax.ShapeDtypeStruct((n, groups, HEAD_DIM, n_h), BF16)] * 2
    return in_specs, out_shape


def _decode(operands, first, count):
    groups = operands[7].shape[0]
    n_h = operands[7].shape[2]
    in_specs, out_shape = _decode_specs(operands, lambda i: (first + i // groups, i % groups))
    out_spec = pl.BlockSpec((None, None, HEAD_DIM, n_h), lambda i: (i // groups, i % groups, 0, 0))
    return pl.pallas_call(
        _decode_kernel,
        grid=(count * groups,),
        in_specs=in_specs,
        out_specs=[out_spec, out_spec],
        out_shape=out_shape(count),
        compiler_params=_cparams(("parallel",)),
        name="decode",
    )(*operands)


def _proj_decode_kernel(x_ref, g_ref, w_ref, *refs):
    *dec_in, z_ref, ya_ref, yb_ref, hn_ref = refs

    @pl.when(pl.program_id(1) == 0)
    def _():
        hn_ref[...] = _rms(x_ref[...], g_ref[...]).astype(BF16)

    _proj_tile(hn_ref[...], w_ref, z_ref)
    _decode_kernel(*dec_in, ya_ref, yb_ref)


def _proj_decode(x, g, w_bf, operands, tm):
    rows, d = x.shape
    n_col = D_IN // PROJ_TN
    steps = rows // tm * n_col
    groups, n_h = operands[7].shape[0], operands[7].shape[2]
    assert groups == 1 and steps <= operands[0].shape[0]
    in_specs, out_shape = _decode_specs(operands, lambda i, j: (i * n_col + j, 0))
    y_spec = pl.BlockSpec((None, None, HEAD_DIM, n_h), lambda i, j: (i * n_col + j, 0, 0, 0))
    return pl.pallas_call(
        _proj_decode_kernel,
        grid=(rows // tm, n_col),
        in_specs=[pl.BlockSpec((tm, d), lambda i, j: (i, 0)),
                  pl.BlockSpec((1, d), lambda i, j: (0, 0)),
                  pl.BlockSpec((d, PROJ_TN), lambda i, j: (0, j))] + in_specs,
        out_specs=[pl.BlockSpec((PROJ_TN // LANES, tm, LANES), lambda i, j: (j, i, 0)), y_spec, y_spec],
        out_shape=[jax.ShapeDtypeStruct((N_CB, rows, LANES), F32)] + out_shape(steps),
        scratch_shapes=[pltpu.VMEM((tm, d), BF16)],
        compiler_params=_cparams(("parallel", "arbitrary")),
        name="proj_decode",
    )(x, g.reshape(1, d), w_bf, *operands)


def _out_kernel(x_ref, ya_ref, yb_ref, p_ref, wout_ref, gple_ref, wpg_ref, wple_ref, gfin_ref, o_ref):
    y = jnp.concatenate([ya_ref[c] for c in range(A_CB)] + [yb_ref[c] for c in range(B_CB)], axis=1)
    h = x_ref[...] + jnp.dot(y, wout_ref[...], preferred_element_type=F32)
    gate = jax.nn.sigmoid(jnp.dot(_rms(h, gple_ref[...]).astype(BF16), wpg_ref[...],
                                  preferred_element_type=F32))
    ple = jnp.dot(p_ref[...].astype(BF16), wple_ref[...], preferred_element_type=F32)
    o_ref[...] = _rms(h + ple * gate, gfin_ref[...])


N_OUT_REFS = 10


def _out_decode_kernel(*refs):
    _out_kernel(*refs[:N_OUT_REFS - 1], refs[-3])
    _decode_kernel(*refs[N_OUT_REFS - 1:-3], refs[-2], refs[-1])


def _out(x, ya, yb, p, wout_bf, g_ple, wpg_bf, wple_bf, g_final, tm, riders=None, first=0):
    rows, d = x.shape
    resident = lambda a: pl.BlockSpec(a.shape, lambda i: (0,) * a.ndim, pipeline_mode=pl.Buffered(1))
    g_ple, g_final = g_ple.reshape(1, d), g_final.reshape(1, d)
    in_specs = [pl.BlockSpec((tm, d), lambda i: (i, 0)),
                pl.BlockSpec((A_CB, tm, LANES), lambda i: (0, i, 0)),
                pl.BlockSpec((B_CB, tm, LANES), lambda i: (0, i, 0)),
                pl.BlockSpec((tm, p.shape[1]), lambda i: (i, 0)),
                resident(wout_bf), resident(g_ple), resident(wpg_bf), resident(wple_bf), resident(g_final)]
    out_specs = [pl.BlockSpec((tm, d), lambda i: (i, 0))]
    out_shape = [jax.ShapeDtypeStruct((rows, d), F32)]
    operands = (x, ya, yb, p, wout_bf, g_ple, wpg_bf, wple_bf, g_final)
    assert len(operands) + 1 == N_OUT_REFS
    if riders is None:
        return pl.pallas_call(_out_kernel, grid=(rows // tm,), in_specs=in_specs, out_specs=out_specs[0],
                              out_shape=out_shape[0], compiler_params=_cparams(("parallel",)), name="out")(*operands)
    groups, n_h = riders[7].shape[0], riders[7].shape[2]
    steps = rows // tm
    assert steps % groups == 0 and first + steps // groups <= riders[0].shape[0]
    rider_specs, rider_shape = _decode_specs(riders, lambda i: (first + i // groups, i % groups))
    y_spec = pl.BlockSpec((None, None, HEAD_DIM, n_h), lambda i: (i // groups, i % groups, 0, 0))
    return pl.pallas_call(
        _out_decode_kernel,
        grid=(steps,),
        in_specs=in_specs + rider_specs,
        out_specs=out_specs + [y_spec, y_spec],
        out_shape=out_shape + rider_shape(steps // groups),
        compiler_params=_cparams(("arbitrary",)),
        name="out_decode",
    )(*operands, *riders)


def _col_blocks(y):
    return y.reshape(y.shape[0], y.shape[1] // LANES, LANES).transpose(1, 0, 2)


def kernel(x_prompt, x_sample, cache_a_k, cache_a_v, cache_b_k, cache_b_v, p_prompt, p_sample,
           g_mix, w_in, sinks, w_out, g_ple, w_pg, w_ple, g_final):
    depth = w_in.shape[0]
    batch, seq, d = x_prompt.shape
    n_s, t_s, _ = x_sample.shape
    assert depth == 1 and t_s == 1 and w_in.shape[2] == D_IN
    xp = x_prompt.reshape(batch * seq, d)
    xs = x_sample.reshape(n_s, d)
    i = 0
    z_s, w_in_bf = _proj(xs, g_mix[i], w_in[i])
    z_s = z_s.transpose(1, 0, 2).reshape(n_s, D_IN)
    dec_whole, dec_halves = (_decode_operands(z_s, cache_a_k[i], cache_a_v[i], cache_b_k[i], cache_b_v[i],
                                              sinks[i], groups) for groups in (1, 2))

    z, *y_s0 = _proj_decode(xp, g_mix[i], w_in_bf, dec_whole, tm=1024)
    ya, ak_t, av_t = _attn_a(z, batch, seq)
    n0 = y_s0[0].shape[0]
    yb, bk_t, bv_t, *rode_b = _attn_b(z, sinks[i], batch, seq, riders=dec_halves, first=n0,
                                      to_bf16=(w_out[i], w_pg[i], w_ple[i]))
    y_s1, (w_out_bf, w_pg_bf, w_ple_bf) = rode_b[:N_DEC_OUT], rode_b[N_DEC_OUT:]
    n1 = n0 + y_s1[0].shape[0]
    y_prompt, *y_s2 = _out(xp, ya, yb, p_prompt[i].reshape(batch * seq, -1), w_out_bf, g_ple[i], w_pg_bf, w_ple_bf,
                           g_final, tm=256, riders=dec_halves, first=n1)
    new_p = [t.transpose(0, 3, 1, 2)[None] for t in (ak_t, av_t, bk_t, bv_t)]

    n2 = n1 + y_s2[0].shape[0]
    rode = [y_s0, y_s1, y_s2] + ([_decode(dec_whole, n2, n_s - n2)] if n2 < n_s else [])
    rows_of = lambda t: t.transpose(0, 1, 3, 2).reshape(t.shape[0], A_WIDTH)
    ya_s, yb_s = (jnp.concatenate([rows_of(t) for t in parts]) for parts in zip(*rode))
    y_sample = _out(xs, _col_blocks(ya_s), _col_blocks(yb_s), p_sample[i].reshape(n_s, -1), w_out_bf, g_ple[i],
                    w_pg_bf, w_ple_bf, g_final, tm=n_s)
    new_s = [z_s[:, off:off + width].reshape(1, n_s, 1, width // HEAD_DIM, HEAD_DIM)
             for off, width in ((KA0, A_WIDTH), (VA0, A_WIDTH), (KB0, B_KV_WIDTH), (VB0, B_KV_WIDTH))]

    return (y_prompt.reshape(batch, seq, d), y_sample.reshape(n_s, t_s, d), *new_p, *new_s)
```
